```python
import jax, jax.numpy as jnp
from jax import lax
import numpy as np

D_MODEL = 1024
BATCH = 32
SEQ = 256
DEPTH = 4
DEC_BATCH = 2
DEC_SEQ = 2048
PAST_LEN = 512

GRID_W = 64
N_MIXERS = 4
N_ATTN = (DEPTH + 3) // 4
N_SGU = (DEPTH + 2) // 4
N_SCONV = (DEPTH + 1) // 4
N_FOURIER = DEPTH // 4
N_HEADS = 16
N_KV_HEADS = 4
HEAD_DIM = 64
Q_PER_KV = N_HEADS // N_KV_HEADS
WINDOW = 128
BLOCK = 128
ROPE_THETA = 10000.0
SGU_CHUNK = 128
SGU_GROUPS = 8
SGU_GROUP_DIM = D_MODEL // SGU_GROUPS
FOURIER_GROUPS = 8
FOURIER_GROUP_DIM = D_MODEL // FOURIER_GROUPS
CONV_W = 3
D_FF = 2816
EPS = 1e-6
NEG_BIG = -1e30

kernel_name = 'hybrid_diffusion_prefix_step'


def rms_norm(x, g):
    xf = x.astype(jnp.float32)
    y = xf * lax.rsqrt(jnp.mean(xf * xf, axis=-1, keepdims=True) + EPS)
    return (y * g.astype(jnp.float32)).astype(x.dtype)


def layer_norm(x, g):
    xf = x.astype(jnp.float32)
    mu = jnp.mean(xf, axis=-1, keepdims=True)
    xc = xf - mu
    y = xc * lax.rsqrt(jnp.mean(xc * xc, axis=-1, keepdims=True) + EPS)
    return (y * g.astype(jnp.float32)).astype(x.dtype)


def modulate(x, g, shift, scale):
    return rms_norm(x, g) * (1 + scale) + shift


def conv3_centred(x, w):
    xp = jnp.pad(x, ((0, 0), (1, 1), (0, 0)))
    return xp[:, :-2] * w[0] + xp[:, 1:-1] * w[1] + xp[:, 2:] * w[2]


def _rotate(x, pos):
    half = x.shape[-1] // 2
    inv = ROPE_THETA ** (-jnp.arange(half, dtype=jnp.float32) / half)
    ang = pos.astype(jnp.float32)[:, None] * inv[None, :]
    cos = jnp.cos(ang)[None, :, None, :]
    sin = jnp.sin(ang)[None, :, None, :]
    x1 = x[..., :half].astype(jnp.float32)
    x2 = x[..., half:].astype(jnp.float32)
    return jnp.concatenate([x1 * cos - x2 * sin, x2 * cos + x1 * sin], axis=-1).astype(x.dtype)


def axial_rope(x):
    t = jnp.arange(x.shape[1])
    row, col = t // GRID_W, t % GRID_W
    h = x.shape[-1] // 2
    return jnp.concatenate([_rotate(x[..., :h], row), _rotate(x[..., h:], col)], axis=-1)


def split_qkv(h, wqkv):
    B, T, _ = h.shape
    qkv = h @ wqkv
    nq, nk = N_HEADS * HEAD_DIM, N_KV_HEADS * HEAD_DIM
    q = qkv[..., :nq].reshape(B, T, N_HEADS, HEAD_DIM)
    k = qkv[..., nq:nq + nk].reshape(B, T, N_KV_HEADS, HEAD_DIM)
    v = qkv[..., nq + nk:].reshape(B, T, N_KV_HEADS, HEAD_DIM)
    return q, k, v


def context_attention(q, k, v, sink):
    B, S = q.shape[:2]
    nq = S // BLOCK
    qb = q.reshape(B, nq, BLOCK, N_KV_HEADS, Q_PER_KV, HEAD_DIM).transpose(1, 0, 2, 3, 4, 5)
    sink_l = sink.astype(jnp.float32).reshape(N_KV_HEADS, Q_PER_KV)[None, :, :, None, None]
    scale = HEAD_DIM ** -0.5

    def one_block(qblk):
        s = jnp.einsum('bqkgd,bskd->bkgqs', qblk, k).astype(jnp.float32) * scale
        s = jnp.concatenate([s, jnp.broadcast_to(sink_l, s.shape[:-1] + (1,))], axis=-1)
        p = jax.nn.softmax(s, axis=-1)[..., :-1].astype(v.dtype)
        return jnp.einsum('bkgqs,bskd->bqkgd', p, v)

    o = lax.map(one_block, qb)
    return o.transpose(1, 0, 2, 3, 4, 5).reshape(B, S, N_HEADS * HEAD_DIM)


def latent_attention(q, k, v, ck, cv, sink):
    B, T = q.shape[:2]
    nb = T // BLOCK
    P = ck.shape[1]
    L = 3 * BLOCK
    qb = q.reshape(B, nb, BLOCK, N_KV_HEADS, Q_PER_KV, HEAD_DIM)

    def band(x):
        xp = jnp.pad(x, ((0, 0), (BLOCK, BLOCK), (0, 0), (0, 0)))
        xp = xp.reshape(B, nb + 2, BLOCK, N_KV_HEADS, HEAD_DIM)
        return jnp.concatenate([xp[:, :-2], xp[:, 1:-1], xp[:, 2:]], axis=2)

    kb, vb = band(k), band(v)
    qpos = jnp.arange(nb)[:, None] * BLOCK + jnp.arange(BLOCK)[None, :]
    kpos = jnp.arange(nb)[:, None] * BLOCK - BLOCK + jnp.arange(L)[None, :]
    rel = kpos[:, None, :] - qpos[:, :, None]
    valid = (jnp.abs(rel) <= WINDOW) & (kpos[:, None, :] >= 0) & (kpos[:, None, :] < T)
    scale = HEAD_DIM ** -0.5
    s_loc = jnp.einsum('bnqkgd,bnskd->bnkgqs', qb, kb).astype(jnp.float32) * scale
    s_loc = jnp.where(valid[None, :, None, None], s_loc, NEG_BIG)
    s_ctx = jnp.einsum('bnqkgd,bskd->bnkgqs', qb, ck).astype(jnp.float32) * scale
    sink_l = sink.astype(jnp.float32).reshape(N_KV_HEADS, Q_PER_KV)[None, None, :, :, None, None]
    s = jnp.concatenate([s_loc, s_ctx, jnp.broadcast_to(sink_l, s_loc.shape[:-1] + (1,))], axis=-1)
    p = jax.nn.softmax(s, axis=-1)
    p_loc = p[..., :L].astype(v.dtype)
    p_ctx = p[..., L:L + P].astype(v.dtype)
    o = (jnp.einsum('bnkgqs,bnskd->bnqkgd', p_loc, vb)
         + jnp.einsum('bnkgqs,bskd->bnqkgd', p_ctx, cv))
    return o.reshape(B, T, N_HEADS * HEAD_DIM)


def attn_mixer_context(h, wqkv, wo, sink):
    q, k, v = split_qkv(h, wqkv)
    return context_attention(q, k, v, sink) @ wo, k, v


def attn_mixer_latent(h, ck, cv, wqkv, wo, sink):
    q, k, v = split_qkv(h, wqkv)
    q, k = axial_rope(q), axial_rope(k)
    return latent_attention(q, k, v, ck, cv, sink) @ wo


def sgu_mixer(h, w_in, ln_g, w_s, b_s, w_out):
    B, T, _ = h.shape
    z = jax.nn.gelu(h @ w_in)
    u, v = z[..., :D_MODEL], z[..., D_MODEL:]
    v = layer_norm(v, ln_g)
    n = T // SGU_CHUNK
    vc = v.reshape(B, n, SGU_CHUNK, SGU_GROUPS, SGU_GROUP_DIM)
    mixed = jnp.einsum('gpr,bnrgc->bnpgc', w_s, vc) + b_s.T[None, None, :, :, None]
    return (u * mixed.reshape(B, T, D_MODEL)) @ w_out


def short_conv_mixer(h, w_in, conv_w, w_out):
    p = h @ w_in
    b, cg, xin = p[..., :D_MODEL], p[..., D_MODEL:2 * D_MODEL], p[..., 2 * D_MODEL:]
    y = conv3_centred(cg * xin, conv_w)
    return (b * y) @ w_out


def fourier_mixer(h, w_out):
    B, T, _ = h.shape
    hg = h.astype(jnp.float32).reshape(B, T, FOURIER_GROUPS, FOURIER_GROUP_DIM)
    f = jnp.fft.fft2(hg, axes=(1, 3), norm='ortho').real
    return f.reshape(B, T, D_MODEL).astype(h.dtype) @ w_out


def conv_ffn(h, w_up, conv_w, w_down):
    a = conv3_centred(h @ w_up, conv_w)
    g, u = a[..., :D_FF], a[..., D_FF:]
    return (jax.nn.silu(g) * u) @ w_down


def setup_inputs(seed: int = 0) -> dict:
    key = jax.random.key(seed)
    ks = jax.random.split(key, 32)
    D = D_MODEL
    qkv_out = (N_HEADS + 2 * N_KV_HEADS) * HEAD_DIM

    def nrm(k, shape, scale):
        return jax.random.normal(k, shape, jnp.float32) * scale

    return {
        'x_prompt': nrm(ks[0], (BATCH, SEQ, D), 1.0),
        'x_sample': nrm(ks[1], (DEC_BATCH, DEC_SEQ, D), 1.0),
        'cache_k': nrm(ks[2], (DEC_BATCH, N_ATTN, PAST_LEN, N_KV_HEADS, HEAD_DIM), 1.0),
        'cache_v': nrm(ks[3], (DEC_BATCH, N_ATTN, PAST_LEN, N_KV_HEADS, HEAD_DIM), 1.0),
        'c': nrm(ks[4], (DEC_BATCH, D), 1.0),
        'c_ctx': nrm(ks[5], (D,), 1.0),
        'ada_w': nrm(ks[6], (DEPTH, D, 6 * D), 0.5 * D ** -0.5),
        'ada_b': nrm(ks[7], (DEPTH, 6 * D), 0.02),
        'norm_mix_g': 1.0 + nrm(ks[8], (DEPTH, D), 0.02),
        'norm_ffn_g': 1.0 + nrm(ks[9], (DEPTH, D), 0.02),
        'final_g': 1.0 + nrm(ks[10], (D,), 0.02),
        'attn_wqkv': nrm(ks[11], (N_ATTN, D, qkv_out), D ** -0.5),
        'attn_wo': nrm(ks[12], (N_ATTN, N_HEADS * HEAD_DIM, D), (N_HEADS * HEAD_DIM) ** -0.5),
        'attn_sink': nrm(ks[13], (N_ATTN, N_HEADS), 0.5),
        'sgu_w_in': nrm(ks[14], (N_SGU, D, 2 * D), D ** -0.5),
        'sgu_ln_g': 1.0 + nrm(ks[15], (N_SGU, D), 0.02),
        'sgu_w_s': nrm(ks[16], (N_SGU, SGU_GROUPS, SGU_CHUNK, SGU_CHUNK), SGU_CHUNK ** -0.5),
        'sgu_b_s': 1.0 + nrm(ks[17], (N_SGU, SGU_GROUPS, SGU_CHUNK), 0.02),
        'sgu_w_out': nrm(ks[18], (N_SGU, D, D), D ** -0.5),
        'sc_w_in': nrm(ks[19], (N_SCONV, D, 3 * D), D ** -0.5),
        'sc_conv': nrm(ks[20], (N_SCONV, CONV_W, D), CONV_W ** -0.5),
        'sc_w_out': nrm(ks[21], (N_SCONV, D, D), D ** -0.5),
        'fn_w_out': nrm(ks[22], (N_FOURIER, D, D), D ** -0.5),
        'ffn_w_up': nrm(ks[23], (DEPTH, D, 2 * D_FF), D ** -0.5),
        'ffn_conv': nrm(ks[24], (DEPTH, CONV_W, 2 * D_FF), CONV_W ** -0.5),
        'ffn_w_down': nrm(ks[25], (DEPTH, D_FF, D), D_FF ** -0.5),
    }


def reference(x_prompt, x_sample, cache_k, cache_v, c, c_ctx, ada_w, ada_b, norm_mix_g,
              norm_ffn_g, final_g, attn_wqkv, attn_wo, attn_sink, sgu_w_in, sgu_ln_g, sgu_w_s,
              sgu_b_s, sgu_w_out, sc_w_in, sc_conv, sc_w_out, fn_w_out, ffn_w_up, ffn_conv,
              ffn_w_down):
    xp, xs = x_prompt, x_sample
    new_k, new_v = [], []
    for l in range(DEPTH):
        kind, j = l % N_MIXERS, l // N_MIXERS
        mod_p = (jax.nn.silu(c_ctx) @ ada_w[l] + ada_b[l])[None, None, :]
        mod_s = (jax.nn.silu(c) @ ada_w[l] + ada_b[l])[:, None, :]
        sh1p, sc1p, g1p, sh2p, sc2p, g2p = jnp.split(mod_p, 6, axis=-1)
        sh1s, sc1s, g1s, sh2s, sc2s, g2s = jnp.split(mod_s, 6, axis=-1)
        hp = modulate(xp, norm_mix_g[l], sh1p, sc1p)
        hs = modulate(xs, norm_mix_g[l], sh1s, sc1s)
        if kind == 0:
            dp, kp, vp = attn_mixer_context(hp, attn_wqkv[j], attn_wo[j], attn_sink[j])
            new_k.append(kp)
            new_v.append(vp)
            ds = attn_mixer_latent(hs, cache_k[:, j], cache_v[:, j], attn_wqkv[j], attn_wo[j],
                                   attn_sink[j])
        elif kind == 1:
            dp = sgu_mixer(hp, sgu_w_in[j], sgu_ln_g[j], sgu_w_s[j], sgu_b_s[j], sgu_w_out[j])
            ds = sgu_mixer(hs, sgu_w_in[j], sgu_ln_g[j], sgu_w_s[j], sgu_b_s[j], sgu_w_out[j])
        elif kind == 2:
            dp = short_conv_mixer(hp, sc_w_in[j], sc_conv[j], sc_w_out[j])
            ds = short_conv_mixer(hs, sc_w_in[j], sc_conv[j], sc_w_out[j])
        else:
            dp = fourier_mixer(hp, fn_w_out[j])
            ds = fourier_mixer(hs, fn_w_out[j])
        xp = xp + g1p * dp
        xs = xs + g1s * ds
        hp = modulate(xp, norm_ffn_g[l], sh2p, sc2p)
        hs = modulate(xs, norm_ffn_g[l], sh2s, sc2s)
        xp = xp + g2p * conv_ffn(hp, ffn_w_up[l], ffn_conv[l], ffn_w_down[l])
        xs = xs + g2s * conv_ffn(hs, ffn_w_up[l], ffn_conv[l], ffn_w_down[l])
    y_prompt = rms_norm(xp, final_g)
    y_sample = rms_norm(xs, final_g)
    new_cache_k = jnp.stack(new_k, axis=1)
    new_cache_v = jnp.stack(new_v, axis=1)
    return (y_prompt, y_sample, new_cache_k, new_cache_v)
```

```python
import functools
import math

import jax
import jax.numpy as jnp
from jax import lax
from jax.experimental import pallas as pl
from jax.experimental.pallas import tpu as pltpu

D = 1024
N_HEADS = 16
N_KV = 4
HD = 64
Q_PER_KV = N_HEADS // N_KV
KV_DIM = N_KV * HD
WINDOW = 128
GRID_W = 64
ROPE_THETA = 10000.0
SGU_CHUNK = 128
SGU_GROUPS = 8
FOURIER_GROUP_DIM = 128
D_FF = 2816
EPS = 1e-6
NEG_BIG = -1e30

LANES = 128
HALO_ROWS = 8
FF_CHUNK = 256
VMEM_LIMIT = 52 * 1024 * 1024

F32 = jnp.float32
BF16 = jnp.bfloat16


def _resident(shape):
    nd = len(shape)
    return pl.BlockSpec(shape, lambda *_: (0,) * nd, pipeline_mode=pl.Buffered(1))


def _params(n_axes):
    return pltpu.CompilerParams(dimension_semantics=("arbitrary",) * n_axes,
                                vmem_limit_bytes=VMEM_LIMIT)


def _modulate(x, g, shift, scale):
    ms = jnp.mean(x * x, axis=-1, keepdims=True)
    return (x * lax.rsqrt(ms + EPS)) * (g * (1.0 + scale)) + shift


def _rms(x, g):
    ms = jnp.mean(x * x, axis=-1, keepdims=True)
    return (x * lax.rsqrt(ms + EPS)) * g


def _dot(a, b):
    return jnp.dot(a, b, preferred_element_type=F32)


def _dot_nt(a, b):
    return lax.dot_general(a, b, (((1,), (1,)), ((), ())), preferred_element_type=F32)


def _ada_kernel(c_ref, w_ref, b_ref, o_ref):
    c = c_ref[...]
    s = (c * jax.nn.sigmoid(c)).astype(BF16)
    o_ref[0] = _dot(s, w_ref[0].astype(BF16)) + b_ref[0]


def _ada_mod(cvec, ada_w, ada_b):
    depth, _, n6 = ada_w.shape
    bn = D
    return pl.pallas_call(
        _ada_kernel,
        grid=(depth, n6 // bn),
        in_specs=[pl.BlockSpec((8, D), lambda l, j: (0, 0)),
                  pl.BlockSpec((1, D, bn), lambda l, j: (l, 0, j)),
                  pl.BlockSpec((1, 1, bn), lambda l, j: (l, 0, j))],
        out_specs=pl.BlockSpec((1, 8, bn), lambda l, j: (l, 0, j)),
        out_shape=jax.ShapeDtypeStruct((depth, 8, n6), F32),
        compiler_params=_params(2),
        name="ada_mod",
    )(cvec, ada_w, ada_b.reshape(depth, 1, n6))


def _halo_specs(n_rows, tm):
    t8 = tm // HALO_ROWS
    last = n_rows // HALO_ROWS - 1
    return [pl.BlockSpec((HALO_ROWS, D), lambda i: (jnp.maximum(i * t8 - 1, 0), 0)),
            pl.BlockSpec((HALO_ROWS, D), lambda i: (jnp.minimum((i + 1) * t8, last), 0))]


def _mod_spec(n_rows, n_groups, tm):
    rows_per_group = n_rows // n_groups
    return pl.BlockSpec((1, 6, D), lambda i: ((i * tm) // rows_per_group, 0, 0))


def _fill_hext(hext_ref, x, xp_ref, xn_ref, g, shift, scale, tm, halo):
    hext_ref[0:tm, :] = _modulate(x, g, shift, scale).astype(BF16)
    if halo:
        hext_ref[tm:tm + HALO_ROWS, :] = _modulate(xp_ref[...], g, shift, scale).astype(BF16)
        hext_ref[tm + HALO_ROWS:tm + 2 * HALO_ROWS, :] = _modulate(
            xn_ref[...], g, shift, scale).astype(BF16)


class _Conv3:
    def __init__(self, tm, seq, halo, width, tile_idx):
        self.tm, self.halo = tm, halo
        rid = lax.broadcasted_iota(jnp.int32, (tm, width), 0)
        if halo:
            self.first = rid == 0
            self.last = rid == tm - 1
            tiles_per_seq = seq // tm
            self.at_seq_start = (tile_idx % tiles_per_seq) == 0
            self.at_seq_end = (tile_idx % tiles_per_seq) == tiles_per_seq - 1
        else:
            self.first = (rid % seq) == 0
            self.last = (rid % seq) == seq - 1

    def __call__(self, a_ext, w):
        tm = self.tm
        a = a_ext[0:tm]
        rolled_prev = pltpu.roll(a, 1, 0)
        rolled_next = pltpu.roll(a, tm - 1, 0)
        if self.halo:
            prev_row = jnp.where(self.at_seq_start, 0.0, a_ext[tm + HALO_ROWS - 1:tm + HALO_ROWS])
            next_row = jnp.where(self.at_seq_end, 0.0, a_ext[tm + HALO_ROWS:tm + HALO_ROWS + 1])
        else:
            prev_row = 0.0
            next_row = 0.0
        a_prev = jnp.where(self.first, prev_row, rolled_prev)
        a_next = jnp.where(self.last, next_row, rolled_next)
        return a_prev * w[0:1] + a * w[1:2] + a_next * w[2:3]


def _tile_layout(n_rows, seq, tm):
    assert n_rows % tm == 0
    halo = seq > tm
    assert (seq % tm == 0) if halo else (tm % seq == 0)
    return halo, tm + 2 * HALO_ROWS if halo else tm


def _ffn_kernel(*refs, tm, seq, halo, final):
    it = iter(refs)
    x_ref = next(it)
    xp_ref = next(it) if halo else None
    xn_ref = next(it) if halo else None
    mod_ref, g_ref, wup_ref, cw_ref, wdn_ref = (next(it) for _ in range(5))
    fg_ref = next(it) if final else None
    o_ref, hext_ref, acc_ref = next(it), next(it), next(it)

    x = x_ref[...]
    mod = mod_ref[0]
    _fill_hext(hext_ref, x, xp_ref, xn_ref, g_ref[...], mod[3:4], mod[4:5], tm, halo)
    conv = _Conv3(tm, seq, halo, FF_CHUNK, pl.program_id(0))

    for j in range(D_FF // FF_CHUNK):
        c0 = j * FF_CHUNK
        hx = hext_ref[...]
        ag = _dot(hx, wup_ref[:, c0:c0 + FF_CHUNK])
        au = _dot(hx, wup_ref[:, D_FF + c0:D_FF + c0 + FF_CHUNK])
        cg = conv(ag, cw_ref[:, c0:c0 + FF_CHUNK])
        cu = conv(au, cw_ref[:, D_FF + c0:D_FF + c0 + FF_CHUNK])
        act = ((cg * jax.nn.sigmoid(cg)) * cu).astype(BF16)
        part = _dot(act, wdn_ref[c0:c0 + FF_CHUNK, :])
        if j == 0:
            acc_ref[...] = part
        else:
            acc_ref[...] += part

    y = x + mod[5:6] * acc_ref[...]
    if final:
        y = _rms(y, fg_ref[...])
    o_ref[...] = y


def _ffn(x, mod, g, wup, cw, wdn, final_g, *, seq, tm):
    n = x.shape[0]
    halo, ext = _tile_layout(n, seq, tm)
    final = final_g is not None
    in_specs = [pl.BlockSpec((tm, D), lambda i: (i, 0))]
    args = [x]
    if halo:
        in_specs += _halo_specs(n, tm)
        args += [x, x]
    in_specs += [_mod_spec(n, mod.shape[0], tm), _resident((1, D)), _resident(wup.shape),
                 _resident(cw.shape), _resident(wdn.shape)]
    args += [mod, g, wup, cw, wdn]
    if final:
        in_specs.append(_resident((1, D)))
        args.append(final_g)
    return pl.pallas_call(
        functools.partial(_ffn_kernel, tm=tm, seq=seq, halo=halo, final=final),
        grid=(n // tm,),
        in_specs=in_specs,
        out_specs=pl.BlockSpec((tm, D), lambda i: (i, 0)),
        out_shape=jax.ShapeDtypeStruct((n, D), F32),
        scratch_shapes=[pltpu.VMEM((ext, D), BF16), pltpu.VMEM((tm, D), F32)],
        compiler_params=_params(1),
        name="conv_ffn",
    )(*args)


def _sconv_kernel(*refs, tm, seq, halo):
    it = iter(refs)
    x_ref = next(it)
    xp_ref = next(it) if halo else None
    xn_ref = next(it) if halo else None
    mod_ref, g_ref, win_ref, cw_ref, wout_ref, o_ref, hext_ref, t_ref = (next(it) for _ in range(8))

    x = x_ref[...]
    mod = mod_ref[0]
    _fill_hext(hext_ref, x, xp_ref, xn_ref, g_ref[...], mod[0:1], mod[1:2], tm, halo)
    conv = _Conv3(tm, seq, halo, FF_CHUNK, pl.program_id(0))

    for j in range(D // FF_CHUNK):
        c0 = j * FF_CHUNK
        hx = hext_ref[...]
        b = _dot(hx, win_ref[:, c0:c0 + FF_CHUNK])
        cg = _dot(hx, win_ref[:, D + c0:D + c0 + FF_CHUNK])
        xin = _dot(hx, win_ref[:, 2 * D + c0:2 * D + c0 + FF_CHUNK])
        y = conv(cg * xin, cw_ref[:, c0:c0 + FF_CHUNK])
        t_ref[:, c0:c0 + FF_CHUNK] = (b[0:tm] * y).astype(BF16)

    o_ref[...] = x + mod[2:3] * _dot(t_ref[...], wout_ref[...])


def _sconv(x, mod, g, win, cw, wout, *, seq, tm):
    n = x.shape[0]
    halo, ext = _tile_layout(n, seq, tm)
    in_specs = [pl.BlockSpec((tm, D), lambda i: (i, 0))]
    args = [x]
    if halo:
        in_specs += _halo_specs(n, tm)
        args += [x, x]
    in_specs += [_mod_spec(n, mod.shape[0], tm), _resident((1, D)), _resident(win.shape),
                 _resident(cw.shape), _resident(wout.shape)]
    args += [mod, g, win, cw, wout]
    return pl.pallas_call(
        functools.partial(_sconv_kernel, tm=tm, seq=seq, halo=halo),
        grid=(n // tm,),
        in_specs=in_specs,
        out_specs=pl.BlockSpec((tm, D), lambda i: (i, 0)),
        out_shape=jax.ShapeDtypeStruct((n, D), F32),
        scratch_shapes=[pltpu.VMEM((ext, D), BF16), pltpu.VMEM((tm, D), BF16)],
        compiler_params=_params(1),
        name="short_conv",
    )(*args)


def _gelu_tanh(x):
    return 0.5 * x * (1.0 + jnp.tanh(math.sqrt(2.0 / math.pi) * (x + 0.044715 * (x * x * x))))


def _sgu_kernel(x_ref, mod_ref, g_ref, win_ref, lng_ref, ws_ref, bs_ref, wout_ref, o_ref,
                v_ref, t_ref, *, tm):
    x = x_ref[...]
    mod = mod_ref[0]
    h = _modulate(x, g_ref[...], mod[0:1], mod[1:2]).astype(BF16)
    v = _gelu_tanh(_dot(h, win_ref[:, D:2 * D]))
    mu = jnp.mean(v, axis=-1, keepdims=True)
    vc = v - mu
    var = jnp.mean(vc * vc, axis=-1, keepdims=True)
    v_ref[...] = ((vc * lax.rsqrt(var + EPS)) * lng_ref[...]).astype(BF16)

    n_chunks = tm // SGU_CHUNK
    for gi in range(SGU_GROUPS):
        cols = slice(gi * LANES, (gi + 1) * LANES)
        u = _gelu_tanh(_dot(h, win_ref[:, cols]))
        rhs = jnp.concatenate(
            [v_ref[n * SGU_CHUNK:(n + 1) * SGU_CHUNK, cols] for n in range(n_chunks)], axis=1)
        mixed = _dot(ws_ref[gi], rhs)
        bias = bs_ref[:, cols]
        for n in range(n_chunks):
            rows = slice(n * SGU_CHUNK, (n + 1) * SGU_CHUNK)
            t_ref[rows, cols] = (u[rows] * (mixed[:, n * LANES:(n + 1) * LANES] + bias)).astype(BF16)

    o_ref[...] = x + mod[2:3] * _dot(t_ref[...], wout_ref[...])


def _sgu(x, mod, g, win, lng, ws, bs_full, wout, *, tm):
    n = x.shape[0]
    return pl.pallas_call(
        functools.partial(_sgu_kernel, tm=tm),
        grid=(n // tm,),
        in_specs=[pl.BlockSpec((tm, D), lambda i: (i, 0)), _mod_spec(n, mod.shape[0], tm),
                  _resident((1, D)), _resident(win.shape), _resident((1, D)), _resident(ws.shape),
                  _resident(bs_full.shape), _resident(wout.shape)],
        out_specs=pl.BlockSpec((tm, D), lambda i: (i, 0)),
        out_shape=jax.ShapeDtypeStruct((n, D), F32),
        scratch_shapes=[pltpu.VMEM((tm, D), BF16), pltpu.VMEM((tm, D), BF16)],
        compiler_params=_params(1),
        name="sgu",
    )(x, mod, g, win, lng, ws, bs_full, wout)


def _softmax_pv(s, sink, v):
    m = jnp.maximum(jnp.max(s, axis=-1, keepdims=True), sink)
    p = jnp.exp(s - m)
    den = jnp.sum(p, axis=-1, keepdims=True) + jnp.exp(sink - m)
    return _dot(p.astype(BF16), v) / den


def _attn_prompt_kernel(x_ref, mod_ref, g_ref, wqkv_ref, wo_ref, sink_ref, o_ref, k_ref, v_ref,
                        q_s, k_s, v_s, o_s, *, bb, seq):
    x = x_ref[...]
    mod = mod_ref[0]
    h = _modulate(x, g_ref[...], mod[0:1], mod[1:2]).astype(BF16)
    nq = N_HEADS * HD
    q_s[...] = (_dot(h, wqkv_ref[:, 0:nq]) * (HD ** -0.5)).astype(BF16)
    k = _dot(h, wqkv_ref[:, nq:nq + KV_DIM])
    v = _dot(h, wqkv_ref[:, nq + KV_DIM:nq + 2 * KV_DIM])
    k_ref[...] = k
    v_ref[...] = v
    k_s[...] = k.astype(BF16)
    v_s[...] = v.astype(BF16)

    def one_batch(b, carry):
        rows = pl.ds(pl.multiple_of(b * seq, seq), seq)
        for kv in range(N_KV):
            kh = k_s[rows, kv * HD:(kv + 1) * HD]
            vh = v_s[rows, kv * HD:(kv + 1) * HD]
            for j in range(Q_PER_KV):
                hd = kv * Q_PER_KV + j
                qh = q_s[rows, hd * HD:(hd + 1) * HD]
                oh = _softmax_pv(_dot_nt(qh, kh), sink_ref[0, hd], vh)
                o_s[rows, hd * HD:(hd + 1) * HD] = oh.astype(BF16)
        return carry

    lax.fori_loop(0, bb, one_batch, 0)
    o_ref[...] = x + mod[2:3] * _dot(o_s[...], wo_ref[...])


def _attn_prompt(x, mod, g, wqkv, wo, sink, *, seq, bb):
    n = x.shape[0]
    tm = bb * seq
    nq = N_HEADS * HD
    return pl.pallas_call(
        functools.partial(_attn_prompt_kernel, bb=bb, seq=seq),
        grid=(n // tm,),
        in_specs=[pl.BlockSpec((tm, D), lambda i: (i, 0)), _mod_spec(n, mod.shape[0], tm),
                  _resident((1, D)), _resident(wqkv.shape), _resident(wo.shape),
                  pl.BlockSpec(memory_space=pltpu.SMEM)],
        out_specs=[pl.BlockSpec((tm, D), lambda i: (i, 0)),
                   pl.BlockSpec((tm, KV_DIM), lambda i: (i, 0)),
                   pl.BlockSpec((tm, KV_DIM), lambda i: (i, 0))],
        out_shape=[jax.ShapeDtypeStruct((n, D), F32),
                   jax.ShapeDtypeStruct((n, KV_DIM), F32),
                   jax.ShapeDtypeStruct((n, KV_DIM), F32)],
        scratch_shapes=[pltpu.VMEM((tm, nq), BF16), pltpu.VMEM((tm, KV_DIM), BF16),
                        pltpu.VMEM((tm, KV_DIM), BF16), pltpu.VMEM((tm, nq), BF16)],
        compiler_params=_params(1),
        name="attn_prompt",
    )(x, mod, g, wqkv, wo, sink)


def _rope_tables(seq):
    t = jnp.arange(seq)
    half = HD // 4
    inv = ROPE_THETA ** (-jnp.arange(half, dtype=F32) / half)
    ang_row = (t // GRID_W).astype(F32)[:, None] * inv[None, :]
    ang_col = (t % GRID_W).astype(F32)[:, None] * inv[None, :]
    cos_h = jnp.concatenate([jnp.cos(ang_row)] * 2 + [jnp.cos(ang_col)] * 2, axis=-1)
    sin_h = jnp.concatenate([-jnp.sin(ang_row), jnp.sin(ang_row),
                             -jnp.sin(ang_col), jnp.sin(ang_col)], axis=-1)
    reps = LANES // HD
    return jnp.tile(cos_h, (1, reps)), jnp.tile(sin_h, (1, reps))


def _qkv_rope_kernel(x_ref, mod_ref, g_ref, wqkv_ref, cos_ref, sin_ref, q_ref, k_ref, v_ref):
    x = x_ref[...]
    mod = mod_ref[0]
    h = _modulate(x, g_ref[...], mod[0:1], mod[1:2]).astype(BF16)
    nq = N_HEADS * HD
    cos = cos_ref[...]
    sin = sin_ref[...]
    lane = lax.broadcasted_iota(jnp.int32, cos.shape, 1)
    low = (lane % (HD // 2)) < (HD // 4)
    for blk in range((nq + KV_DIM) // LANES):
        c0 = blk * LANES
        a = _dot(h, wqkv_ref[:, c0:c0 + LANES])
        partner = jnp.where(low, pltpu.roll(a, LANES - HD // 4, 1), pltpu.roll(a, HD // 4, 1))
        r = a * cos + partner * sin
        if c0 < nq:
            q_ref[:, c0:c0 + LANES] = (r * (HD ** -0.5)).astype(BF16)
        else:
            k_ref[:, c0 - nq:c0 - nq + LANES] = r.astype(BF16)
    v_ref[...] = _dot(h, wqkv_ref[:, nq + KV_DIM:nq + 2 * KV_DIM]).astype(BF16)


def _qkv_rope(x, mod, g, wqkv, cos, sin, *, seq, tm):
    n = x.shape[0]
    nq = N_HEADS * HD
    tiles_per_seq = seq // tm
    return pl.pallas_call(
        _qkv_rope_kernel,
        grid=(n // tm,),
        in_specs=[pl.BlockSpec((tm, D), lambda i: (i, 0)), _mod_spec(n, mod.shape[0], tm),
                  _resident((1, D)), _resident(wqkv.shape),
                  pl.BlockSpec((tm, LANES), lambda i: (i % tiles_per_seq, 0)),
                  pl.BlockSpec((tm, LANES), lambda i: (i % tiles_per_seq, 0))],
        out_specs=[pl.BlockSpec((tm, nq), lambda i: (i, 0)),
                   pl.BlockSpec((tm, KV_DIM), lambda i: (i, 0)),
                   pl.BlockSpec((tm, KV_DIM), lambda i: (i, 0))],
        out_shape=[jax.ShapeDtypeStruct((n, nq), BF16),
                   jax.ShapeDtypeStruct((n, KV_DIM), BF16),
                   jax.ShapeDtypeStruct((n, KV_DIM), BF16)],
        compiler_params=_params(1),
        name="qkv_rope",
    )(x, mod, g, wqkv, cos, sin)


def _attn_latent_kernel(x_ref, mod_ref, q_ref, k_ref, v_ref, ck_ref, cv_ref, wo_ref, sink_ref,
                        o_ref, o_s, *, tq, seq, span):
    qt = pl.program_id(1)
    past = ck_ref.shape[1]
    ws = pl.multiple_of(jnp.clip(qt * tq - WINDOW, 0, seq - span), WINDOW)
    qpos = qt * tq + lax.broadcasted_iota(jnp.int32, (tq, span + past), 0)
    col = lax.broadcasted_iota(jnp.int32, (tq, span + past), 1)
    valid = (jnp.abs(qpos - (ws + col)) <= WINDOW) | (col >= span)

    for kv in range(N_KV):
        lanes = slice(kv * HD, (kv + 1) * HD)
        kcat = jnp.concatenate([k_ref[0, pl.ds(ws, span), lanes], ck_ref[0, :, lanes]], axis=0)
        vcat = jnp.concatenate([v_ref[0, pl.ds(ws, span), lanes], cv_ref[0, :, lanes]], axis=0)
        for j in range(Q_PER_KV):
            hd = kv * Q_PER_KV + j
            s = jnp.where(valid, _dot_nt(q_ref[0, :, hd * HD:(hd + 1) * HD], kcat), NEG_BIG)
            o_s[:, hd * HD:(hd + 1) * HD] = _softmax_pv(s, sink_ref[0, hd], vcat).astype(BF16)

    mod = mod_ref[0]
    o_ref[0] = x_ref[0] + mod[2:3] * _dot(o_s[...], wo_ref[...])


def _attn_latent(x, mod, q, k, v, ck, cv, wo, sink, *, tq):
    nb, seq, _ = x.shape
    nq = N_HEADS * HD
    past = ck.shape[1]
    span = tq + 2 * WINDOW
    return pl.pallas_call(
        functools.partial(_attn_latent_kernel, tq=tq, seq=seq, span=span),
        grid=(nb, seq // tq),
        in_specs=[pl.BlockSpec((1, tq, D), lambda b, t: (b, t, 0)),
                  pl.BlockSpec((1, 6, D), lambda b, t: (b, 0, 0)),
                  pl.BlockSpec((1, tq, nq), lambda b, t: (b, t, 0)),
                  pl.BlockSpec((1, seq, KV_DIM), lambda b, t: (b, 0, 0)),
                  pl.BlockSpec((1, seq, KV_DIM), lambda b, t: (b, 0, 0)),
                  pl.BlockSpec((1, past, KV_DIM), lambda b, t: (b, 0, 0)),
                  pl.BlockSpec((1, past, KV_DIM), lambda b, t: (b, 0, 0)),
                  _resident(wo.shape),
                  pl.BlockSpec(memory_space=pltpu.SMEM)],
        out_specs=pl.BlockSpec((1, tq, D), lambda b, t: (b, t, 0)),
        out_shape=jax.ShapeDtypeStruct((nb, seq, D), F32),
        scratch_shapes=[pltpu.VMEM((tq, nq), BF16)],
        compiler_params=_params(2),
        name="attn_latent",
    )(x, mod, q, k, v, ck, cv, wo, sink)


def _dft_angle(k, t, period):
    return ((k * t) % period).astype(F32) * (2.0 * math.pi / period)


def _fourier_tables(seq):
    c = jnp.arange(FOURIER_GROUP_DIM, dtype=jnp.int32)
    ang_c = _dft_angle(c[:, None], c[None, :], FOURIER_GROUP_DIM)
    chan = jnp.concatenate([jnp.cos(ang_c), jnp.sin(ang_c)], axis=1).astype(BF16)
    k = jnp.arange(seq, dtype=jnp.int32)
    hi = jnp.arange(seq // LANES, dtype=jnp.int32) * LANES
    lo = jnp.arange(LANES, dtype=jnp.int32)
    ang_hi = _dft_angle(k[:, None], hi[None, :], seq)
    ang_lo = _dft_angle(k[:, None], lo[None, :], seq)
    return chan, jnp.cos(ang_hi), jnp.sin(ang_hi), jnp.cos(ang_lo), jnp.sin(ang_lo)


def _fourier_chan_kernel(x_ref, mod_ref, g_ref, chan_ref, o_ref, *, bb, tt):
    x = x_ref[...].reshape(bb * tt, D)
    mod = mod_ref[0]
    h = _modulate(x, g_ref[...], mod[0:1], mod[1:2]).astype(BF16)
    for gi in range(D // FOURIER_GROUP_DIM):
        cols = slice(gi * LANES, (gi + 1) * LANES)
        cs = _dot(h[:, cols], chan_ref[...]).astype(BF16)
        for b in range(bb):
            rows = slice(b * tt, (b + 1) * tt)
            o_ref[b, 0, :, cols] = cs[rows, 0:LANES]
            o_ref[b, 1, :, cols] = cs[rows, LANES:2 * LANES]


def _fourier_chan(x, mod, g, chan, *, bb, tt):
    nb, seq, _ = x.shape
    n_groups = mod.shape[0]
    return pl.pallas_call(
        functools.partial(_fourier_chan_kernel, bb=bb, tt=tt),
        grid=(nb // bb, seq // tt),
        in_specs=[pl.BlockSpec((bb, tt, D), lambda b, t: (b, t, 0)),
                  pl.BlockSpec((1, 6, D), lambda b, t: ((b * bb * n_groups) // nb, 0, 0)),
                  _resident((1, D)), _resident(chan.shape)],
        out_specs=pl.BlockSpec((bb, 2, tt, D), lambda b, t: (b, 0, t, 0)),
        out_shape=jax.ShapeDtypeStruct((nb, 2, seq, D), BF16),
        compiler_params=_params(2),
        name="fourier_chan",
    )(x, mod, g, chan)


def _fourier_pos_kernel(x_ref, mod_ref, hh_ref, chi_ref, shi_ref, clo_ref, slo_ref, wout_ref,
                        o_ref, lhs_ref, *, seq, scale):
    clo = clo_ref[...]
    slo = slo_ref[...]
    for a in range(seq // LANES):
        ca = chi_ref[:, a:a + 1]
        sa = shi_ref[:, a:a + 1]
        lhs_ref[:, a * LANES:(a + 1) * LANES] = (ca * clo - sa * slo).astype(BF16)
        lhs_ref[:, seq + a * LANES:seq + (a + 1) * LANES] = (-(sa * clo + ca * slo)).astype(BF16)
    f = (_dot(lhs_ref[...], hh_ref[0]) * scale).astype(BF16)
    mod = mod_ref[0]
    o_ref[0] = x_ref[0] + mod[2:3] * _dot(f, wout_ref[...])


def _fourier_pos(x, mod, hh, chi, shi, clo, slo, wout, *, tr):
    nb, seq, _ = x.shape
    n_groups = mod.shape[0]
    n_hi = seq // LANES
    scale = 1.0 / math.sqrt(seq * FOURIER_GROUP_DIM)
    return pl.pallas_call(
        functools.partial(_fourier_pos_kernel, seq=seq, scale=scale),
        grid=(nb, seq // tr),
        in_specs=[pl.BlockSpec((1, tr, D), lambda b, t: (b, t, 0)),
                  pl.BlockSpec((1, 6, D), lambda b, t: ((b * n_groups) // nb, 0, 0)),
                  pl.BlockSpec((1, 2 * seq, D), lambda b, t: (b, 0, 0)),
                  pl.BlockSpec((tr, n_hi), lambda b, t: (t, 0)),
                  pl.BlockSpec((tr, n_hi), lambda b, t: (t, 0)),
                  pl.BlockSpec((tr, LANES), lambda b, t: (t, 0)),
                  pl.BlockSpec((tr, LANES), lambda b, t: (t, 0)),
                  _resident(wout.shape)],
        out_specs=pl.BlockSpec((1, tr, D), lambda b, t: (b, t, 0)),
        out_shape=jax.ShapeDtypeStruct((nb, seq, D), F32),
        scratch_shapes=[pltpu.VMEM((tr, 2 * seq), BF16)],
        compiler_params=_params(2),
        name="fourier_pos",
    )(x, mod, hh, chi, shi, clo, slo, wout)


def _fourier(x, mod, g, wout, *, bb, tt, tr):
    nb, seq, _ = x.shape
    chan, chi, shi, clo, slo = _fourier_tables(seq)
    hh = _fourier_chan(x, mod, g, chan, bb=bb, tt=tt)
    return _fourier_pos(x, mod, hh.reshape(nb, 2 * seq, D), chi, shi, clo, slo, wout, tr=tr)


def kernel(x_prompt, x_sample, cache_k, cache_v, c, c_ctx, ada_w, ada_b, norm_mix_g, norm_ffn_g,
           final_g, attn_wqkv, attn_wo, attn_sink, sgu_w_in, sgu_ln_g, sgu_w_s, sgu_b_s,
           sgu_w_out, sc_w_in, sc_conv, sc_w_out, fn_w_out, ffn_w_up, ffn_conv, ffn_w_down):
    nbp, seq_p, _ = x_prompt.shape
    nbs, seq_s, _ = x_sample.shape
    depth = ada_w.shape[0]
    tm = 512

    cvec = jnp.concatenate([c_ctx[None, :], c, jnp.zeros((8 - 1 - nbs, D), F32)], axis=0)
    mods = _ada_mod(cvec, ada_w, ada_b).reshape(depth, 8, 6, D)

    xp = x_prompt.reshape(nbp * seq_p, D)
    xs = x_sample.reshape(nbs * seq_s, D)
    new_k = new_v = None

    for l in range(depth):
        kind, j = l % 4, l // 4
        mp = mods[l, 0:1]
        ms = mods[l, 1:1 + nbs]
        g_mix = norm_mix_g[l][None, :]
        if kind == 0:
            wqkv = attn_wqkv[j].astype(BF16)
            wo = attn_wo[j].astype(BF16)
            sink = attn_sink[j][None, :]
            xp, new_k, new_v = _attn_prompt(xp, mp, g_mix, wqkv, wo, sink, seq=seq_p, bb=2)
            cos, sin = _rope_tables(seq_s)
            q, k, v = _qkv_rope(xs, ms, g_mix, wqkv, cos, sin, seq=seq_s, tm=tm)
            past = cache_k.shape[2]
            ck = cache_k[:, j].reshape(nbs, past, KV_DIM).astype(BF16)
            cv = cache_v[:, j].reshape(nbs, past, KV_DIM).astype(BF16)
            xs = _attn_latent(xs.reshape(nbs, seq_s, D), ms, q.reshape(nbs, seq_s, N_HEADS * HD),
                              k.reshape(nbs, seq_s, KV_DIM), v.reshape(nbs, seq_s, KV_DIM),
                              ck, cv, wo, sink, tq=256).reshape(nbs * seq_s, D)
        elif kind == 1:
            win = sgu_w_in[j].astype(BF16)
            ws = sgu_w_s[j].astype(BF16)
            wout = sgu_w_out[j].astype(BF16)
            lng = sgu_ln_g[j][None, :]
            bs_full = jnp.repeat(sgu_b_s[j].T, LANES, axis=1)
            xp = _sgu(xp, mp, g_mix, win, lng, ws, bs_full, wout, tm=tm)
            xs = _sgu(xs, ms, g_mix, win, lng, ws, bs_full, wout, tm=tm)
        elif kind == 2:
            win = sc_w_in[j].astype(BF16)
            wout = sc_w_out[j].astype(BF16)
            xp = _sconv(xp, mp, g_mix, win, sc_conv[j], wout, seq=seq_p, tm=tm)
            xs = _sconv(xs, ms, g_mix, win, sc_conv[j], wout, seq=seq_s, tm=tm)
        else:
            wout = fn_w_out[j].astype(BF16)
            xp = _fourier(xp.reshape(nbp, seq_p, D), mp, g_mix, wout,
                          bb=2, tt=seq_p, tr=seq_p).reshape(nbp * seq_p, D)
            xs = _fourier(xs.reshape(nbs, seq_s, D), ms, g_mix, wout,
                          bb=1, tt=tm, tr=tm).reshape(nbs * seq_s, D)

        wup = ffn_w_up[l].astype(BF16)
        wdn = ffn_w_down[l].astype(BF16)
        g_ffn = norm_ffn_g[l][None, :]
        fg = final_g[None, :] if l == depth - 1 else None
        xp = _ffn(xp, mp, g_ffn, wup, ffn_conv[l], wdn, fg, seq=seq_p, tm=tm)
        xs = _ffn(xs, ms, g_ffn, wup, ffn_conv[l], wdn, fg, seq=seq_s, tm=tm)

    y_prompt = xp.reshape(nbp, seq_p, D)
    y_sample = xs.reshape(nbs, seq_s, D)
    new_cache_k = new_k.reshape(nbp, 1, seq_p, N_KV, HD)
    new_cache_v = new_v.reshape(nbp, 1, seq_p, N_KV, HD)
    return (y_prompt, y_sample, new_cache_k, new_cache_v)
```

```python
import functools
import math

import jax
import jax.numpy as jnp
from jax import lax
from jax.experimental import pallas as pl
from jax.experimental.pallas import tpu as pltpu

D = 1024
N_HEADS = 16
N_KV = 4
HD = 64
Q_PER_KV = N_HEADS // N_KV
KV_DIM = N_KV * HD
WINDOW = 128
GRID_W = 64
ROPE_THETA = 10000.0
SGU_CHUNK = 128
SGU_GROUPS = 8
FOURIER_GROUP_DIM = 128
D_FF = 2816
EPS = 1e-6
NEG_BIG = -1e30

LANES = 128
HALO_ROWS = 8
FF_CHUNK = 256
UNIT = 256
VMEM_LIMIT = 52 * 1024 * 1024

F32 = jnp.float32
BF16 = jnp.bfloat16


def _resident(shape):
    nd = len(shape)
    return pl.BlockSpec(shape, lambda *_: (0,) * nd, pipeline_mode=pl.Buffered(1))


def _params(n_axes, flags=None):
    return pltpu.CompilerParams(dimension_semantics=("arbitrary",) * n_axes,
                                vmem_limit_bytes=VMEM_LIMIT, flags=flags)


def _modulate(x, g, shift, scale):
    ms = jnp.mean(x * x, axis=-1, keepdims=True)
    return (x * lax.rsqrt(ms + EPS)) * (g * (1.0 + scale)) + shift


def _rms(x, g):
    ms = jnp.mean(x * x, axis=-1, keepdims=True)
    return (x * lax.rsqrt(ms + EPS)) * g


def _dot(a, b):
    return jnp.dot(a, b, preferred_element_type=F32)


def _dot_nt(a, b):
    return lax.dot_general(a, b, (((1,), (1,)), ((), ())), preferred_element_type=F32)


def _ada_kernel(c_ref, w_ref, b_ref, o_ref):
    c = c_ref[...]
    s = (c * jax.nn.sigmoid(c)).astype(BF16)
    o_ref[0] = _dot(s, w_ref[0].astype(BF16)) + b_ref[0]


def _ada_mod(cvec, ada_w, ada_b):
    depth, _, n6 = ada_w.shape
    bn = D
    return pl.pallas_call(
        _ada_kernel,
        grid=(depth, n6 // bn),
        in_specs=[pl.BlockSpec((8, D), lambda l, j: (0, 0)),
                  pl.BlockSpec((1, D, bn), lambda l, j: (l, 0, j)),
                  pl.BlockSpec((1, 1, bn), lambda l, j: (l, 0, j))],
        out_specs=pl.BlockSpec((1, 8, bn), lambda l, j: (l, 0, j)),
        out_shape=jax.ShapeDtypeStruct((depth, 8, n6), F32),
        compiler_params=_params(2),
        name="ada_mod",
    )(cvec, ada_w, ada_b.reshape(depth, 1, n6))


def _halo_specs(n_rows, tm):
    t8 = tm // HALO_ROWS
    last = n_rows // HALO_ROWS - 1
    return [pl.BlockSpec((HALO_ROWS, D), lambda i: (jnp.maximum(i * t8 - 1, 0), 0)),
            pl.BlockSpec((HALO_ROWS, D), lambda i: (jnp.minimum((i + 1) * t8, last), 0))]


def _mod_spec(n_rows, n_groups, tm):
    rows_per_group = n_rows // n_groups
    return pl.BlockSpec((1, 6, D), lambda i: ((i * tm) // rows_per_group, 0, 0))


class _Units:
    def __init__(self, n_rows, seq, tm):
        assert n_rows % tm == 0 and tm % UNIT == 0
        self.tm, self.seq = tm, seq
        self.n = tm // UNIT
        self.halo = seq > UNIT
        assert seq == UNIT or (seq > tm and seq % tm == 0)
        self.ext_h = UNIT + 2 * HALO_ROWS if self.halo else UNIT
        self.ext_a = UNIT + 2 * HALO_ROWS

    def fill_hext(self, hext_ref, x, xp_ref, xn_ref, g, shift, scale, tile_idx):
        h = _modulate(x, g, shift, scale)
        for u in range(self.n):
            hext_ref[u, 0:UNIT, :] = h[u * UNIT:(u + 1) * UNIT].astype(BF16)
        if not self.halo:
            return
        tiles_per_seq = self.seq // self.tm
        at_start = (tile_idx % tiles_per_seq) == 0
        at_end = (tile_idx % tiles_per_seq) == tiles_per_seq - 1
        for u in range(self.n):
            if u > 0:
                prev = h[u * UNIT - HALO_ROWS:u * UNIT]
            else:
                prev = jnp.where(at_start, 0.0, _modulate(xp_ref[...], g, shift, scale))
            if u < self.n - 1:
                nxt = h[(u + 1) * UNIT:(u + 1) * UNIT + HALO_ROWS]
            else:
                nxt = jnp.where(at_end, 0.0, _modulate(xn_ref[...], g, shift, scale))
            hext_ref[u, UNIT:UNIT + HALO_ROWS, :] = prev.astype(BF16)
            hext_ref[u, UNIT + HALO_ROWS:UNIT + 2 * HALO_ROWS, :] = nxt.astype(BF16)


    def zero_gaps(self, ref):
        zeros = jnp.zeros((HALO_ROWS, LANES), ref.dtype)
        for c in range(ref.shape[0]):
            ref[c, 0:HALO_ROWS, :] = zeros
            ref[c, HALO_ROWS + UNIT:2 * HALO_ROWS + UNIT, :] = zeros

    def store_padded(self, ref, r):
        for c in range(ref.shape[0]):
            cols = slice(c * LANES, (c + 1) * LANES)
            ref[c, HALO_ROWS:HALO_ROWS + UNIT, :] = r[0:UNIT, cols]
            if self.halo:
                ref[c, 0:HALO_ROWS, :] = r[UNIT:UNIT + HALO_ROWS, cols]
                ref[c, HALO_ROWS + UNIT:2 * HALO_ROWS + UNIT, :] = (
                    r[UNIT + HALO_ROWS:UNIT + 2 * HALO_ROWS, cols])

    def conv3(self, ref, w):
        b = HALO_ROWS
        out = []
        for c in range(ref.shape[0]):
            wc = w[:, c * LANES:(c + 1) * LANES]
            out.append(ref[c, b - 1:b - 1 + UNIT, :] * wc[0:1] + ref[c, b:b + UNIT, :] * wc[1:2]
                       + ref[c, b + 1:b + 1 + UNIT, :] * wc[2:3])
        return jnp.concatenate(out, axis=1)


def _ffn_kernel(*refs, units, final):
    halo, n_units = units.halo, units.n
    it = iter(refs)
    x_ref = next(it)
    xp_ref = next(it) if halo else None
    xn_ref = next(it) if halo else None
    mod_ref, g_ref, wup_ref, cw_ref, wdn_ref = (next(it) for _ in range(5))
    fg_ref = next(it) if final else None
    o_ref, hext_ref, acc_ref = (next(it) for _ in range(3))
    a_refs = [[[next(it), next(it)] for _ in range(n_units)] for _ in range(2)]

    x = x_ref[...]
    mod = mod_ref[0]
    units.fill_hext(hext_ref, x, xp_ref, xn_ref, g_ref[...], mod[3:4], mod[4:5], pl.program_id(0))
    if not halo:
        for p in range(2):
            for u in range(n_units):
                for t in range(2):
                    units.zero_gaps(a_refs[p][u][t])

    def up(u, j):
        c0 = j * FF_CHUNK
        hx = hext_ref[u]
        for t, col in enumerate((c0, D_FF + c0)):
            units.store_padded(a_refs[j % 2][u][t], _dot(hx, wup_ref[:, col:col + FF_CHUNK]))

    n_chunks = D_FF // FF_CHUNK
    for u in range(n_units):
        up(u, 0)
    for j in range(n_chunks):
        c0 = j * FF_CHUNK
        p = j % 2
        for u in range(n_units):
            if j + 1 < n_chunks:
                up(u, j + 1)
            cg = units.conv3(a_refs[p][u][0], cw_ref[:, c0:c0 + FF_CHUNK])
            cu = units.conv3(a_refs[p][u][1], cw_ref[:, D_FF + c0:D_FF + c0 + FF_CHUNK])
            act = ((cg * jax.nn.sigmoid(cg)) * cu).astype(BF16)
            part = _dot(act, wdn_ref[c0:c0 + FF_CHUNK, :])
            rows = slice(u * UNIT, (u + 1) * UNIT)
            if j == 0:
                acc_ref[rows, :] = part
            else:
                acc_ref[rows, :] += part

    y = x + mod[5:6] * acc_ref[...]
    if final:
        y = _rms(y, fg_ref[...])
    o_ref[...] = y


def _layer_resident(arr, layer):
    nd = arr.ndim - 1
    return pl.BlockSpec((None,) + arr.shape[1:], lambda *_: (layer,) + (0,) * nd,
                        pipeline_mode=pl.Buffered(1))


def _ffn(x, mod, g, wup, cw, wdn, final_g, *, layer, seq, tm):
    n = x.shape[0]
    units = _Units(n, seq, tm)
    halo = units.halo
    final = final_g is not None
    in_specs = [pl.BlockSpec((tm, D), lambda i: (i, 0))]
    args = [x]
    if halo:
        in_specs += _halo_specs(n, tm)
        args += [x, x]
    in_specs += [_mod_spec(n, mod.shape[0], tm), _layer_resident(g, layer),
                 _layer_resident(wup, layer), _layer_resident(cw, layer),
                 _layer_resident(wdn, layer)]
    args += [mod, g, wup, cw, wdn]
    if final:
        in_specs.append(_resident((1, D)))
        args.append(final_g)
    scratch = [pltpu.VMEM((units.n, units.ext_h, D), BF16), pltpu.VMEM((tm, D), F32)]
    scratch += [pltpu.VMEM((FF_CHUNK // LANES, units.ext_a, LANES), F32)] * (4 * units.n)
    return pl.pallas_call(
        functools.partial(_ffn_kernel, units=units, final=final),
        grid=(n // tm,),
        in_specs=in_specs,
        out_specs=pl.BlockSpec((tm, D), lambda i: (i, 0)),
        out_shape=jax.ShapeDtypeStruct((n, D), F32),
        scratch_shapes=scratch,
        compiler_params=_params(1),
        name="conv_ffn",
    )(*args)


def _sconv_kernel(*refs, units):
    halo, n_units = units.halo, units.n
    it = iter(refs)
    x_ref = next(it)
    xp_ref = next(it) if halo else None
    xn_ref = next(it) if halo else None
    mod_ref, g_ref, win_ref, cw_ref, wout_ref, o_ref, hext_ref, t_ref = (next(it) for _ in range(8))
    p_refs = [[next(it) for _ in range(n_units)] for _ in range(2)]

    x = x_ref[...]
    mod = mod_ref[0]
    units.fill_hext(hext_ref, x, xp_ref, xn_ref, g_ref[...], mod[0:1], mod[1:2], pl.program_id(0))
    if not halo:
        for p in range(2):
            for u in range(n_units):
                units.zero_gaps(p_refs[p][u])

    for j in range(D // FF_CHUNK):
        c0 = j * FF_CHUNK
        for u in range(n_units):
            hx = hext_ref[u]
            b = _dot(hx, win_ref[:, c0:c0 + FF_CHUNK])
            cg = _dot(hx, win_ref[:, D + c0:D + c0 + FF_CHUNK])
            xin = _dot(hx, win_ref[:, 2 * D + c0:2 * D + c0 + FF_CHUNK])
            units.store_padded(p_refs[j % 2][u], cg * xin)
            y = units.conv3(p_refs[j % 2][u], cw_ref[:, c0:c0 + FF_CHUNK])
            t_ref[u * UNIT:(u + 1) * UNIT, c0:c0 + FF_CHUNK] = (b[0:UNIT] * y).astype(BF16)

    o_ref[...] = x + mod[2:3] * _dot(t_ref[...], wout_ref[...])


def _sconv(x, mod, g, win, cw, wout, *, seq, tm):
    n = x.shape[0]
    units = _Units(n, seq, tm)
    halo = units.halo
    in_specs = [pl.BlockSpec((tm, D), lambda i: (i, 0))]
    args = [x]
    if halo:
        in_specs += _halo_specs(n, tm)
        args += [x, x]
    in_specs += [_mod_spec(n, mod.shape[0], tm), _resident((1, D)), _resident(win.shape),
                 _resident(cw.shape), _resident(wout.shape)]
    args += [mod, g, win, cw, wout]
    scratch = [pltpu.VMEM((units.n, units.ext_h, D), BF16), pltpu.VMEM((tm, D), BF16)]
    scratch += [pltpu.VMEM((FF_CHUNK // LANES, units.ext_a, LANES), F32)] * (2 * units.n)
    return pl.pallas_call(
        functools.partial(_sconv_kernel, units=units),
        grid=(n // tm,),
        in_specs=in_specs,
        out_specs=pl.BlockSpec((tm, D), lambda i: (i, 0)),
        out_shape=jax.ShapeDtypeStruct((n, D), F32),
        scratch_shapes=scratch,
        compiler_params=_params(1),
        name="short_conv",
    )(*args)


def _gelu_tanh(x):
    return 0.5 * x * (1.0 + jnp.tanh(math.sqrt(2.0 / math.pi) * (x + 0.044715 * (x * x * x))))


def _sgu_kernel(x_ref, mod_ref, g_ref, win_ref, lng_ref, ws_ref, bs_ref, wout_ref, o_ref,
                v_ref, t_ref, *, tm):
    x = x_ref[...]
    mod = mod_ref[0]
    h = _modulate(x, g_ref[...], mod[0:1], mod[1:2]).astype(BF16)
    v = _gelu_tanh(_dot(h, win_ref[:, D:2 * D]))
    mu = jnp.mean(v, axis=-1, keepdims=True)
    vc = v - mu
    var = jnp.mean(vc * vc, axis=-1, keepdims=True)
    v_ref[...] = ((vc * lax.rsqrt(var + EPS)) * lng_ref[...]).astype(BF16)

    n_chunks = tm // SGU_CHUNK
    for gi in range(SGU_GROUPS):
        cols = slice(gi * LANES, (gi + 1) * LANES)
        u = _gelu_tanh(_dot(h, win_ref[:, cols]))
        rhs = jnp.concatenate(
            [v_ref[n * SGU_CHUNK:(n + 1) * SGU_CHUNK, cols] for n in range(n_chunks)], axis=1)
        mixed = _dot(ws_ref[gi], rhs)
        bias = bs_ref[:, cols]
        for n in range(n_chunks):
            rows = slice(n * SGU_CHUNK, (n + 1) * SGU_CHUNK)
            t_ref[rows, cols] = (u[rows] * (mixed[:, n * LANES:(n + 1) * LANES] + bias)).astype(BF16)

    o_ref[...] = x + mod[2:3] * _dot(t_ref[...], wout_ref[...])


def _sgu(x, mod, g, win, lng, ws, bs_full, wout, *, tm):
    n = x.shape[0]
    return pl.pallas_call(
        functools.partial(_sgu_kernel, tm=tm),
        grid=(n // tm,),
        in_specs=[pl.BlockSpec((tm, D), lambda i: (i, 0)), _mod_spec(n, mod.shape[0], tm),
                  _resident((1, D)), _resident(win.shape), _resident((1, D)), _resident(ws.shape),
                  _resident(bs_full.shape), _resident(wout.shape)],
        out_specs=pl.BlockSpec((tm, D), lambda i: (i, 0)),
        out_shape=jax.ShapeDtypeStruct((n, D), F32),
        scratch_shapes=[pltpu.VMEM((tm, D), BF16), pltpu.VMEM((tm, D), BF16)],
        compiler_params=_params(1),
        name="sgu",
    )(x, mod, g, win, lng, ws, bs_full, wout)


def _softmax_pv(s, sink, v):
    m = jnp.maximum(jnp.max(s, axis=-1, keepdims=True), sink)
    p = jnp.exp(s - m)
    den = jnp.sum(p, axis=-1, keepdims=True) + jnp.exp(sink - m)
    return _dot(p.astype(BF16), v) / den


def _attn_prompt_kernel(x_ref, mod_ref, g_ref, wqkv_ref, wo_ref, sink_ref, o_ref, k_ref, v_ref,
                        q_s, k_s, v_s, o_s, *, bb, seq):
    x = x_ref[...]
    mod = mod_ref[0]
    h = _modulate(x, g_ref[...], mod[0:1], mod[1:2]).astype(BF16)
    nq = N_HEADS * HD
    q_s[...] = (_dot(h, wqkv_ref[:, 0:nq]) * (HD ** -0.5)).astype(BF16)
    k = _dot(h, wqkv_ref[:, nq:nq + KV_DIM])
    v = _dot(h, wqkv_ref[:, nq + KV_DIM:nq + 2 * KV_DIM])
    k_ref[...] = k
    v_ref[...] = v
    k_s[...] = k.astype(BF16)
    v_s[...] = v.astype(BF16)

    def one_batch(b, carry):
        rows = pl.ds(pl.multiple_of(b * seq, seq), seq)
        for kv in range(N_KV):
            kh = k_s[rows, kv * HD:(kv + 1) * HD]
            vh = v_s[rows, kv * HD:(kv + 1) * HD]
            for j in range(Q_PER_KV):
                hd = kv * Q_PER_KV + j
                qh = q_s[rows, hd * HD:(hd + 1) * HD]
                oh = _softmax_pv(_dot_nt(qh, kh), sink_ref[0, hd], vh)
                o_s[rows, hd * HD:(hd + 1) * HD] = oh.astype(BF16)
        return carry

    lax.fori_loop(0, bb, one_batch, 0)
    o_ref[...] = x + mod[2:3] * _dot(o_s[...], wo_ref[...])


def _attn_prompt(x, mod, g, wqkv, wo, sink, *, seq, bb):
    n = x.shape[0]
    tm = bb * seq
    nq = N_HEADS * HD
    return pl.pallas_call(
        functools.partial(_attn_prompt_kernel, bb=bb, seq=seq),
        grid=(n // tm,),
        in_specs=[pl.BlockSpec((tm, D), lambda i: (i, 0)), _mod_spec(n, mod.shape[0], tm),
                  _resident((1, D)), _resident(wqkv.shape), _resident(wo.shape),
                  pl.BlockSpec(memory_space=pltpu.SMEM)],
        out_specs=[pl.BlockSpec((tm, D), lambda i: (i, 0)),
                   pl.BlockSpec((tm, KV_DIM), lambda i: (i, 0)),
                   pl.BlockSpec((tm, KV_DIM), lambda i: (i, 0))],
        out_shape=[jax.ShapeDtypeStruct((n, D), F32),
                   jax.ShapeDtypeStruct((n, KV_DIM), F32),
                   jax.ShapeDtypeStruct((n, KV_DIM), F32)],
        scratch_shapes=[pltpu.VMEM((tm, nq), BF16), pltpu.VMEM((tm, KV_DIM), BF16),
                        pltpu.VMEM((tm, KV_DIM), BF16), pltpu.VMEM((tm, nq), BF16)],
        compiler_params=_params(1),
        name="attn_prompt",
    )(x, mod, g, wqkv, wo, sink)


def _rope_tables(seq):
    t = jnp.arange(seq)
    half = HD // 4
    inv = ROPE_THETA ** (-jnp.arange(half, dtype=F32) / half)
    ang_row = (t // GRID_W).astype(F32)[:, None] * inv[None, :]
    ang_col = (t % GRID_W).astype(F32)[:, None] * inv[None, :]
    cos_h = jnp.concatenate([jnp.cos(ang_row)] * 2 + [jnp.cos(ang_col)] * 2, axis=-1)
    sin_h = jnp.concatenate([-jnp.sin(ang_row), jnp.sin(ang_row),
                             -jnp.sin(ang_col), jnp.sin(ang_col)], axis=-1)
    reps = LANES // HD
    return jnp.tile(cos_h, (1, reps)), jnp.tile(sin_h, (1, reps))


def _qkv_rope_kernel(x_ref, mod_ref, g_ref, wqkv_ref, cos_ref, sin_ref, q_ref, k_ref, v_ref):
    x = x_ref[...]
    mod = mod_ref[0]
    h = _modulate(x, g_ref[...], mod[0:1], mod[1:2]).astype(BF16)
    nq = N_HEADS * HD
    cos = cos_ref[...]
    sin = sin_ref[...]
    lane = lax.broadcasted_iota(jnp.int32, cos.shape, 1)
    low = (lane % (HD // 2)) < (HD // 4)
    for blk in range((nq + KV_DIM) // LANES):
        c0 = blk * LANES
        a = _dot(h, wqkv_ref[:, c0:c0 + LANES])
        partner = jnp.where(low, pltpu.roll(a, LANES - HD // 4, 1), pltpu.roll(a, HD // 4, 1))
        r = a * cos + partner * sin
        if c0 < nq:
            q_ref[:, c0:c0 + LANES] = (r * (HD ** -0.5)).astype(BF16)
        else:
            k_ref[:, c0 - nq:c0 - nq + LANES] = r.astype(BF16)
    v_ref[...] = _dot(h, wqkv_ref[:, nq + KV_DIM:nq + 2 * KV_DIM]).astype(BF16)


def _qkv_rope(x, mod, g, wqkv, cos, sin, *, seq, tm):
    n = x.shape[0]
    nq = N_HEADS * HD
    tiles_per_seq = seq // tm
    return pl.pallas_call(
        _qkv_rope_kernel,
        grid=(n // tm,),
        in_specs=[pl.BlockSpec((tm, D), lambda i: (i, 0)), _mod_spec(n, mod.shape[0], tm),
                  _resident((1, D)), _resident(wqkv.shape),
                  pl.BlockSpec((tm, LANES), lambda i: (i % tiles_per_seq, 0)),
                  pl.BlockSpec((tm, LANES), lambda i: (i % tiles_per_seq, 0))],
        out_specs=[pl.BlockSpec((tm, nq), lambda i: (i, 0)),
                   pl.BlockSpec((tm, KV_DIM), lambda i: (i, 0)),
                   pl.BlockSpec((tm, KV_DIM), lambda i: (i, 0))],
        out_shape=[jax.ShapeDtypeStruct((n, nq), BF16),
                   jax.ShapeDtypeStruct((n, KV_DIM), BF16),
                   jax.ShapeDtypeStruct((n, KV_DIM), BF16)],
        compiler_params=_params(1),
        name="qkv_rope",
    )(x, mod, g, wqkv, cos, sin)


def _attn_latent_kernel(x_ref, mod_ref, q_ref, k_ref, v_ref, ck_ref, cv_ref, wo_ref, sink_ref,
                        o_ref, o_s, *, tq, seq, span):
    qt = pl.program_id(1)
    past = ck_ref.shape[1]
    ws = pl.multiple_of(jnp.clip(qt * tq - WINDOW, 0, seq - span), WINDOW)
    qpos = qt * tq + lax.broadcasted_iota(jnp.int32, (tq, span + past), 0)
    col = lax.broadcasted_iota(jnp.int32, (tq, span + past), 1)
    valid = (jnp.abs(qpos - (ws + col)) <= WINDOW) | (col >= span)

    for kv in range(N_KV):
        lanes = slice(kv * HD, (kv + 1) * HD)
        kcat = jnp.concatenate([k_ref[0, pl.ds(ws, span), lanes], ck_ref[0, :, lanes]], axis=0)
        vcat = jnp.concatenate([v_ref[0, pl.ds(ws, span), lanes], cv_ref[0, :, lanes]], axis=0)
        for j in range(Q_PER_KV):
            hd = kv * Q_PER_KV + j
            s = jnp.where(valid, _dot_nt(q_ref[0, :, hd * HD:(hd + 1) * HD], kcat), NEG_BIG)
            o_s[:, hd * HD:(hd + 1) * HD] = _softmax_pv(s, sink_ref[0, hd], vcat).astype(BF16)

    mod = mod_ref[0]
    o_ref[0] = x_ref[0] + mod[2:3] * _dot(o_s[...], wo_ref[...])


def _attn_latent(x, mod, q, k, v, ck, cv, wo, sink, *, tq):
    nb, seq, _ = x.shape
    nq = N_HEADS * HD
    past = ck.shape[1]
    span = tq + 2 * WINDOW
    return pl.pallas_call(
        functools.partial(_attn_latent_kernel, tq=tq, seq=seq, span=span),
        grid=(nb, seq // tq),
        in_specs=[pl.BlockSpec((1, tq, D), lambda b, t: (b, t, 0)),
                  pl.BlockSpec((1, 6, D), lambda b, t: (b, 0, 0)),
                  pl.BlockSpec((1, tq, nq), lambda b, t: (b, t, 0)),
                  pl.BlockSpec((1, seq, KV_DIM), lambda b, t: (b, 0, 0)),
                  pl.BlockSpec((1, seq, KV_DIM), lambda b, t: (b, 0, 0)),
                  pl.BlockSpec((1, past, KV_DIM), lambda b, t: (b, 0, 0)),
                  pl.BlockSpec((1, past, KV_DIM), lambda b, t: (b, 0, 0)),
                  _resident(wo.shape),
                  pl.BlockSpec(memory_space=pltpu.SMEM)],
        out_specs=pl.BlockSpec((1, tq, D), lambda b, t: (b, t, 0)),
        out_shape=jax.ShapeDtypeStruct((nb, seq, D), F32),
        scratch_shapes=[pltpu.VMEM((tq, nq), BF16)],
        compiler_params=_params(2),
        name="attn_latent",
    )(x, mod, q, k, v, ck, cv, wo, sink)


def _dft_angle(k, t, period):
    return ((k * t) % period).astype(F32) * (2.0 * math.pi / period)


def _fourier_tables(seq):
    c = jnp.arange(FOURIER_GROUP_DIM, dtype=jnp.int32)
    ang_c = _dft_angle(c[:, None], c[None, :], FOURIER_GROUP_DIM)
    chan = jnp.concatenate([jnp.cos(ang_c), jnp.sin(ang_c)], axis=1).astype(BF16)
    k = jnp.arange(seq, dtype=jnp.int32)
    hi = jnp.arange(seq // LANES, dtype=jnp.int32) * LANES
    lo = jnp.arange(LANES, dtype=jnp.int32)
    ang_hi = _dft_angle(k[:, None], hi[None, :], seq)
    ang_lo = _dft_angle(k[:, None], lo[None, :], seq)
    return chan, jnp.cos(ang_hi), jnp.sin(ang_hi), jnp.cos(ang_lo), jnp.sin(ang_lo)


def _fourier_chan_kernel(x_ref, mod_ref, g_ref, chan_ref, o_ref, *, bb, tt):
    x = x_ref[...].reshape(bb * tt, D)
    mod = mod_ref[0]
    h = _modulate(x, g_ref[...], mod[0:1], mod[1:2]).astype(BF16)
    for gi in range(D // FOURIER_GROUP_DIM):
        cols = slice(gi * LANES, (gi + 1) * LANES)
        cs = _dot(h[:, cols], chan_ref[...]).astype(BF16)
        for b in range(bb):
            rows = slice(b * tt, (b + 1) * tt)
            o_ref[b, 0, :, cols] = cs[rows, 0:LANES]
            o_ref[b, 1, :, cols] = cs[rows, LANES:2 * LANES]


def _fourier_chan(x, mod, g, chan, *, bb, tt):
    nb, seq, _ = x.shape
    n_groups = mod.shape[0]
    return pl.pallas_call(
        functools.partial(_fourier_chan_kernel, bb=bb, tt=tt),
        grid=(nb // bb, seq // tt),
        in_specs=[pl.BlockSpec((bb, tt, D), lambda b, t: (b, t, 0)),
                  pl.BlockSpec((1, 6, D), lambda b, t: ((b * bb * n_groups) // nb, 0, 0)),
                  _resident((1, D)), _resident(chan.shape)],
        out_specs=pl.BlockSpec((bb, 2, tt, D), lambda b, t: (b, 0, t, 0)),
        out_shape=jax.ShapeDtypeStruct((nb, 2, seq, D), BF16),
        compiler_params=_params(2),
        name="fourier_chan",
    )(x, mod, g, chan)


def _fourier_pos_kernel(x_ref, mod_ref, hh_ref, chi_ref, shi_ref, clo_ref, slo_ref, wout_ref,
                        o_ref, lhs_ref, *, seq, scale):
    clo = clo_ref[...]
    slo = slo_ref[...]
    for a in range(seq // LANES):
        ca = chi_ref[:, a:a + 1]
        sa = shi_ref[:, a:a + 1]
        lhs_ref[:, a * LANES:(a + 1) * LANES] = (ca * clo - sa * slo).astype(BF16)
        lhs_ref[:, seq + a * LANES:seq + (a + 1) * LANES] = (-(sa * clo + ca * slo)).astype(BF16)
    f = (_dot(lhs_ref[...], hh_ref[0]) * scale).astype(BF16)
    mod = mod_ref[0]
    o_ref[0] = x_ref[0] + mod[2:3] * _dot(f, wout_ref[...])


def _fourier_pos(x, mod, hh, chi, shi, clo, slo, wout, *, tr):
    nb, seq, _ = x.shape
    n_groups = mod.shape[0]
    n_hi = seq // LANES
    scale = 1.0 / math.sqrt(seq * FOURIER_GROUP_DIM)
    return pl.pallas_call(
        functools.partial(_fourier_pos_kernel, seq=seq, scale=scale),
        grid=(nb, seq // tr),
        in_specs=[pl.BlockSpec((1, tr, D), lambda b, t: (b, t, 0)),
                  pl.BlockSpec((1, 6, D), lambda b, t: ((b * n_groups) // nb, 0, 0)),
                  pl.BlockSpec((1, 2 * seq, D), lambda b, t: (b, 0, 0)),
                  pl.BlockSpec((tr, n_hi), lambda b, t: (t, 0)),
                  pl.BlockSpec((tr, n_hi), lambda b, t: (t, 0)),
                  pl.BlockSpec((tr, LANES), lambda b, t: (t, 0)),
                  pl.BlockSpec((tr, LANES), lambda b, t: (t, 0)),
                  _resident(wout.shape)],
        out_specs=pl.BlockSpec((1, tr, D), lambda b, t: (b, t, 0)),
        out_shape=jax.ShapeDtypeStruct((nb, seq, D), F32),
        scratch_shapes=[pltpu.VMEM((tr, 2 * seq), BF16)],
        compiler_params=_params(2),
        name="fourier_pos",
    )(x, mod, hh, chi, shi, clo, slo, wout)


def _fourier(x, mod, g, wout, *, bb, tt, tr):
    nb, seq, _ = x.shape
    chan, chi, shi, clo, slo = _fourier_tables(seq)
    hh = _fourier_chan(x, mod, g, chan, bb=bb, tt=tt)
    return _fourier_pos(x, mod, hh.reshape(nb, 2 * seq, D), chi, shi, clo, slo, wout, tr=tr)


def kernel(x_prompt, x_sample, cache_k, cache_v, c, c_ctx, ada_w, ada_b, norm_mix_g, norm_ffn_g,
           final_g, attn_wqkv, attn_wo, attn_sink, sgu_w_in, sgu_ln_g, sgu_w_s, sgu_b_s,
           sgu_w_out, sc_w_in, sc_conv, sc_w_out, fn_w_out, ffn_w_up, ffn_conv, ffn_w_down):
    nbp, seq_p, _ = x_prompt.shape
    nbs, seq_s, _ = x_sample.shape
    depth = ada_w.shape[0]
    tm = 512

    cvec = jnp.concatenate([c_ctx[None, :], c, jnp.zeros((8 - 1 - nbs, D), F32)], axis=0)
    mods = _ada_mod(cvec, ada_w, ada_b).reshape(depth, 8, 6, D)

    xp = x_prompt.reshape(nbp * seq_p, D)
    xs = x_sample.reshape(nbs * seq_s, D)
    new_k = new_v = None
    wup = ffn_w_up.astype(BF16)
    wdn = ffn_w_down.astype(BF16)
    g_ffn = norm_ffn_g[:, None, :]

    for l in range(depth):
        kind, j = l % 4, l // 4
        mp = mods[l, 0:1]
        ms = mods[l, 1:1 + nbs]
        g_mix = norm_mix_g[l][None, :]
        if kind == 0:
            wqkv = attn_wqkv[j].astype(BF16)
            wo = attn_wo[j].astype(BF16)
            sink = attn_sink[j][None, :]
            xp, new_k, new_v = _attn_prompt(xp, mp, g_mix, wqkv, wo, sink, seq=seq_p, bb=2)
            cos, sin = _rope_tables(seq_s)
            q, k, v = _qkv_rope(xs, ms, g_mix, wqkv, cos, sin, seq=seq_s, tm=tm)
            past = cache_k.shape[2]
            ck = cache_k[:, j].reshape(nbs, past, KV_DIM).astype(BF16)
            cv = cache_v[:, j].reshape(nbs, past, KV_DIM).astype(BF16)
            xs = _attn_latent(xs.reshape(nbs, seq_s, D), ms, q.reshape(nbs, seq_s, N_HEADS * HD),
                              k.reshape(nbs, seq_s, KV_DIM), v.reshape(nbs, seq_s, KV_DIM),
                              ck, cv, wo, sink, tq=256).reshape(nbs * seq_s, D)
        elif kind == 1:
            win = sgu_w_in[j].astype(BF16)
            ws = sgu_w_s[j].astype(BF16)
            wout = sgu_w_out[j].astype(BF16)
            lng = sgu_ln_g[j][None, :]
            bs_full = jnp.repeat(sgu_b_s[j].T, LANES, axis=1)
            xp = _sgu(xp, mp, g_mix, win, lng, ws, bs_full, wout, tm=tm)
            xs = _sgu(xs, ms, g_mix, win, lng, ws, bs_full, wout, tm=tm)
        elif kind == 2:
            win = sc_w_in[j].astype(BF16)
            wout = sc_w_out[j].astype(BF16)
            xp = _sconv(xp, mp, g_mix, win, sc_conv[j], wout, seq=seq_p, tm=tm)
            xs = _sconv(xs, ms, g_mix, win, sc_conv[j], wout, seq=seq_s, tm=tm)
        else:
            wout = fn_w_out[j].astype(BF16)
            xp = _fourier(xp.reshape(nbp, seq_p, D), mp, g_mix, wout,
                          bb=2, tt=seq_p, tr=seq_p).reshape(nbp * seq_p, D)
            xs = _fourier(xs.reshape(nbs, seq_s, D), ms, g_mix, wout,
                          bb=1, tt=tm, tr=tm).reshape(nbs * seq_s, D)

        fg = final_g[None, :] if l == depth - 1 else None
        xp = _ffn(xp, mp, g_ffn, wup, ffn_conv, wdn, fg, layer=l, seq=seq_p, tm=tm)
        xs = _ffn(xs, ms, g_ffn, wup, ffn_conv, wdn, fg, layer=l, seq=seq_s, tm=tm)

    y_prompt = xp.reshape(nbp, seq_p, D)
    y_sample = xs.reshape(nbs, seq_s, D)
    new_cache_k = new_k.reshape(nbp, 1, seq_p, N_KV, HD)
    new_cache_v = new_v.reshape(nbp, 1, seq_p, N_KV, HD)
    return (y_prompt, y_sample, new_cache_k, new_cache_v)
```

```python
import functools
import math

import jax
import jax.numpy as jnp
from jax import lax
from jax.experimental import pallas as pl
from jax.experimental.pallas import tpu as pltpu

D = 1024
N_HEADS = 16
N_KV = 4
HD = 64
Q_PER_KV = N_HEADS // N_KV
KV_DIM = N_KV * HD
WINDOW = 128
GRID_W = 64
ROPE_THETA = 10000.0
SGU_CHUNK = 128
SGU_GROUPS = 8
FOURIER_GROUP_DIM = 128
D_FF = 2816
EPS = 1e-6
NEG_BIG = -1e30

LANES = 128
HALO_ROWS = 8
FF_CHUNK = 256
UNIT = 256
VMEM_LIMIT = 52 * 1024 * 1024

F32 = jnp.float32
BF16 = jnp.bfloat16


def _resident(shape):
    nd = len(shape)
    return pl.BlockSpec(shape, lambda *_: (0,) * nd, pipeline_mode=pl.Buffered(1))


def _params(n_axes, flags=None):
    return pltpu.CompilerParams(dimension_semantics=("arbitrary",) * n_axes,
                                vmem_limit_bytes=VMEM_LIMIT, flags=flags)


def _modulate(x, g, shift, scale):
    ms = jnp.mean(x * x, axis=-1, keepdims=True)
    return (x * lax.rsqrt(ms + EPS)) * (g * (1.0 + scale)) + shift


def _rms(x, g):
    ms = jnp.mean(x * x, axis=-1, keepdims=True)
    return (x * lax.rsqrt(ms + EPS)) * g


def _dot(a, b):
    return jnp.dot(a, b, preferred_element_type=F32)


def _dot_nt(a, b):
    return lax.dot_general(a, b, (((1,), (1,)), ((), ())), preferred_element_type=F32)


def _ada_kernel(c_ref, w_ref, b_ref, o_ref):
    c = c_ref[...]
    s = (c * jax.nn.sigmoid(c)).astype(BF16)
    o_ref[0] = _dot(s, w_ref[0].astype(BF16)) + b_ref[0]


def _ada_mod(cvec, ada_w, ada_b):
    depth, _, n6 = ada_w.shape
    bn = D
    return pl.pallas_call(
        _ada_kernel,
        grid=(depth, n6 // bn),
        in_specs=[pl.BlockSpec((8, D), lambda l, j: (0, 0)),
                  pl.BlockSpec((1, D, bn), lambda l, j: (l, 0, j)),
                  pl.BlockSpec((1, 1, bn), lambda l, j: (l, 0, j))],
        out_specs=pl.BlockSpec((1, 8, bn), lambda l, j: (l, 0, j)),
        out_shape=jax.ShapeDtypeStruct((depth, 8, n6), F32),
        compiler_params=_params(2),
        name="ada_mod",
    )(cvec, ada_w, ada_b.reshape(depth, 1, n6))


def _halo_specs(n_rows, tm):
    t8 = tm // HALO_ROWS
    last = n_rows // HALO_ROWS - 1
    return [pl.BlockSpec((HALO_ROWS, D), lambda i: (jnp.maximum(i * t8 - 1, 0), 0)),
            pl.BlockSpec((HALO_ROWS, D), lambda i: (jnp.minimum((i + 1) * t8, last), 0))]


def _mod_spec(n_rows, n_groups, tm):
    rows_per_group = n_rows // n_groups
    return pl.BlockSpec((1, 6, D), lambda i: ((i * tm) // rows_per_group, 0, 0))


class _Units:
    def __init__(self, n_rows, seq, tm):
        assert n_rows % tm == 0 and tm % UNIT == 0
        self.tm, self.seq = tm, seq
        self.n = tm // UNIT
        self.halo = seq > UNIT
        assert seq == UNIT or (seq > tm and seq % tm == 0)
        self.ext_h = UNIT + 2 * HALO_ROWS if self.halo else UNIT
        self.ext_a = UNIT + 2 * HALO_ROWS

    def fill_hext(self, hext_ref, x, xp_ref, xn_ref, g, shift, scale, tile_idx):
        h = _modulate(x, g, shift, scale)
        for u in range(self.n):
            hext_ref[u, 0:UNIT, :] = h[u * UNIT:(u + 1) * UNIT].astype(BF16)
        if not self.halo:
            return
        tiles_per_seq = self.seq // self.tm
        at_start = (tile_idx % tiles_per_seq) == 0
        at_end = (tile_idx % tiles_per_seq) == tiles_per_seq - 1
        for u in range(self.n):
            if u > 0:
                prev = h[u * UNIT - HALO_ROWS:u * UNIT]
            else:
                prev = jnp.where(at_start, 0.0, _modulate(xp_ref[...], g, shift, scale))
            if u < self.n - 1:
                nxt = h[(u + 1) * UNIT:(u + 1) * UNIT + HALO_ROWS]
            else:
                nxt = jnp.where(at_end, 0.0, _modulate(xn_ref[...], g, shift, scale))
            hext_ref[u, UNIT:UNIT + HALO_ROWS, :] = prev.astype(BF16)
            hext_ref[u, UNIT + HALO_ROWS:UNIT + 2 * HALO_ROWS, :] = nxt.astype(BF16)


    def zero_gaps(self, ref):
        zeros = jnp.zeros((HALO_ROWS, LANES), ref.dtype)
        for c in range(ref.shape[0]):
            ref[c, 0:HALO_ROWS, :] = zeros
            ref[c, HALO_ROWS + UNIT:2 * HALO_ROWS + UNIT, :] = zeros

    def store_padded(self, ref, r):
        for c in range(ref.shape[0]):
            cols = slice(c * LANES, (c + 1) * LANES)
            ref[c, HALO_ROWS:HALO_ROWS + UNIT, :] = r[0:UNIT, cols]
            if self.halo:
                ref[c, 0:HALO_ROWS, :] = r[UNIT:UNIT + HALO_ROWS, cols]
                ref[c, HALO_ROWS + UNIT:2 * HALO_ROWS + UNIT, :] = (
                    r[UNIT + HALO_ROWS:UNIT + 2 * HALO_ROWS, cols])

    def conv3(self, ref, w):
        b = HALO_ROWS
        out = []
        for c in range(ref.shape[0]):
            wc = w[:, c * LANES:(c + 1) * LANES]
            out.append(ref[c, b - 1:b - 1 + UNIT, :] * wc[0:1] + ref[c, b:b + UNIT, :] * wc[1:2]
                       + ref[c, b + 1:b + 1 + UNIT, :] * wc[2:3])
        return jnp.concatenate(out, axis=1)


def _ffn_kernel(*refs, units, final):
    halo, n_units = units.halo, units.n
    it = iter(refs)
    x_ref = next(it)
    xp_ref = next(it) if halo else None
    xn_ref = next(it) if halo else None
    mod_ref, g_ref, wup_ref, cw_ref, wdn_ref = (next(it) for _ in range(5))
    fg_ref = next(it) if final else None
    o_ref, hext_ref, acc_ref = (next(it) for _ in range(3))
    a_refs = [[[next(it), next(it)] for _ in range(n_units)] for _ in range(2)]

    x = x_ref[...]
    mod = mod_ref[0]
    units.fill_hext(hext_ref, x, xp_ref, xn_ref, g_ref[...], mod[3:4], mod[4:5], pl.program_id(0))
    if not halo:
        for p in range(2):
            for u in range(n_units):
                for t in range(2):
                    units.zero_gaps(a_refs[p][u][t])

    def up(u, j):
        c0 = j * FF_CHUNK
        hx = hext_ref[u]
        for t, col in enumerate((c0, D_FF + c0)):
            units.store_padded(a_refs[j % 2][u][t], _dot(hx, wup_ref[:, col:col + FF_CHUNK]))

    n_chunks = D_FF // FF_CHUNK
    for u in range(n_units):
        up(u, 0)
    for j in range(n_chunks):
        c0 = j * FF_CHUNK
        p = j % 2
        for u in range(n_units):
            if j + 1 < n_chunks:
                up(u, j + 1)
            cg = units.conv3(a_refs[p][u][0], cw_ref[:, c0:c0 + FF_CHUNK])
            cu = units.conv3(a_refs[p][u][1], cw_ref[:, D_FF + c0:D_FF + c0 + FF_CHUNK])
            act = ((cg * jax.nn.sigmoid(cg)) * cu).astype(BF16)
            part = _dot(act, wdn_ref[c0:c0 + FF_CHUNK, :])
            rows = slice(u * UNIT, (u + 1) * UNIT)
            if j == 0:
                acc_ref[rows, :] = part
            else:
                acc_ref[rows, :] += part

    y = x + mod[5:6] * acc_ref[...]
    if final:
        y = _rms(y, fg_ref[...])
    o_ref[...] = y


def _layer_resident(arr, layer):
    nd = arr.ndim - 1
    return pl.BlockSpec((None,) + arr.shape[1:], lambda *_: (layer,) + (0,) * nd,
                        pipeline_mode=pl.Buffered(1))


def _ffn(x, mod, g, wup, cw, wdn, final_g, *, layer, seq, tm):
    n = x.shape[0]
    units = _Units(n, seq, tm)
    halo = units.halo
    final = final_g is not None
    in_specs = [pl.BlockSpec((tm, D), lambda i: (i, 0))]
    args = [x]
    if halo:
        in_specs += _halo_specs(n, tm)
        args += [x, x]
    in_specs += [_mod_spec(n, mod.shape[0], tm), _layer_resident(g, layer),
                 _layer_resident(wup, layer), _layer_resident(cw, layer),
                 _layer_resident(wdn, layer)]
    args += [mod, g, wup, cw, wdn]
    if final:
        in_specs.append(_resident((1, D)))
        args.append(final_g)
    scratch = [pltpu.VMEM((units.n, units.ext_h, D), BF16), pltpu.VMEM((tm, D), F32)]
    scratch += [pltpu.VMEM((FF_CHUNK // LANES, units.ext_a, LANES), F32)] * (4 * units.n)
    return pl.pallas_call(
        functools.partial(_ffn_kernel, units=units, final=final),
        grid=(n // tm,),
        in_specs=in_specs,
        out_specs=pl.BlockSpec((tm, D), lambda i: (i, 0)),
        out_shape=jax.ShapeDtypeStruct((n, D), F32),
        scratch_shapes=scratch,
        compiler_params=_params(1),
        name="conv_ffn",
    )(*args)


def _sconv_kernel(*refs, units):
    halo, n_units = units.halo, units.n
    it = iter(refs)
    x_ref = next(it)
    xp_ref = next(it) if halo else None
    xn_ref = next(it) if halo else None
    mod_ref, g_ref, win_ref, cw_ref, wout_ref, o_ref, hext_ref, t_ref = (next(it) for _ in range(8))
    p_refs = [[next(it) for _ in range(n_units)] for _ in range(2)]

    x = x_ref[...]
    mod = mod_ref[0]
    units.fill_hext(hext_ref, x, xp_ref, xn_ref, g_ref[...], mod[0:1], mod[1:2], pl.program_id(0))
    if not halo:
        for p in range(2):
            for u in range(n_units):
                units.zero_gaps(p_refs[p][u])

    for j in range(D // FF_CHUNK):
        c0 = j * FF_CHUNK
        for u in range(n_units):
            hx = hext_ref[u]
            b = _dot(hx, win_ref[:, c0:c0 + FF_CHUNK])
            cg = _dot(hx, win_ref[:, D + c0:D + c0 + FF_CHUNK])
            xin = _dot(hx, win_ref[:, 2 * D + c0:2 * D + c0 + FF_CHUNK])
            units.store_padded(p_refs[j % 2][u], cg * xin)
            y = units.conv3(p_refs[j % 2][u], cw_ref[:, c0:c0 + FF_CHUNK])
            t_ref[u * UNIT:(u + 1) * UNIT, c0:c0 + FF_CHUNK] = (b[0:UNIT] * y).astype(BF16)

    o_ref[...] = x + mod[2:3] * _dot(t_ref[...], wout_ref[...])


def _sconv(x, mod, g, win, cw, wout, *, seq, tm):
    n = x.shape[0]
    units = _Units(n, seq, tm)
    halo = units.halo
    in_specs = [pl.BlockSpec((tm, D), lambda i: (i, 0))]
    args = [x]
    if halo:
        in_specs += _halo_specs(n, tm)
        args += [x, x]
    in_specs += [_mod_spec(n, mod.shape[0], tm), _resident((1, D)), _resident(win.shape),
                 _resident(cw.shape), _resident(wout.shape)]
    args += [mod, g, win, cw, wout]
    scratch = [pltpu.VMEM((units.n, units.ext_h, D), BF16), pltpu.VMEM((tm, D), BF16)]
    scratch += [pltpu.VMEM((FF_CHUNK // LANES, units.ext_a, LANES), F32)] * (2 * units.n)
    return pl.pallas_call(
        functools.partial(_sconv_kernel, units=units),
        grid=(n // tm,),
        in_specs=in_specs,
        out_specs=pl.BlockSpec((tm, D), lambda i: (i, 0)),
        out_shape=jax.ShapeDtypeStruct((n, D), F32),
        scratch_shapes=scratch,
        compiler_params=_params(1),
        name="short_conv",
    )(*args)


def _gelu_tanh(x):
    return 0.5 * x * (1.0 + jnp.tanh(math.sqrt(2.0 / math.pi) * (x + 0.044715 * (x * x * x))))


def _sgu_kernel(x_ref, mod_ref, g_ref, win_ref, lng_ref, ws_ref, bs_ref, wout_ref, o_ref,
                v_ref, t_ref, *, tm):
    x = x_ref[...]
    mod = mod_ref[0]
    h = _modulate(x, g_ref[...], mod[0:1], mod[1:2]).astype(BF16)
    v = _gelu_tanh(_dot(h, win_ref[:, D:2 * D]))
    mu = jnp.mean(v, axis=-1, keepdims=True)
    vc = v - mu
    var = jnp.mean(vc * vc, axis=-1, keepdims=True)
    v_ref[...] = ((vc * lax.rsqrt(var + EPS)) * lng_ref[...]).astype(BF16)

    n_chunks = tm // SGU_CHUNK
    for gi in range(SGU_GROUPS):
        cols = slice(gi * LANES, (gi + 1) * LANES)
        if gi % 2 == 0:
            u_pair = _gelu_tanh(_dot(h, win_ref[:, gi * LANES:(gi + 2) * LANES]))
        u = u_pair[:, (gi % 2) * LANES:(gi % 2 + 1) * LANES]
        rhs = jnp.concatenate(
            [v_ref[n * SGU_CHUNK:(n + 1) * SGU_CHUNK, cols] for n in range(n_chunks)], axis=1)
        mixed = _dot(ws_ref[gi], rhs)
        bias = bs_ref[:, cols]
        for n in range(n_chunks):
            rows = slice(n * SGU_CHUNK, (n + 1) * SGU_CHUNK)
            t_ref[rows, cols] = (u[rows] * (mixed[:, n * LANES:(n + 1) * LANES] + bias)).astype(BF16)

    o_ref[...] = x + mod[2:3] * _dot(t_ref[...], wout_ref[...])


def _sgu(x, mod, g, win, lng, ws, bs_full, wout, *, tm):
    n = x.shape[0]
    return pl.pallas_call(
        functools.partial(_sgu_kernel, tm=tm),
        grid=(n // tm,),
        in_specs=[pl.BlockSpec((tm, D), lambda i: (i, 0)), _mod_spec(n, mod.shape[0], tm),
                  _resident((1, D)), _resident(win.shape), _resident((1, D)), _resident(ws.shape),
                  _resident(bs_full.shape), _resident(wout.shape)],
        out_specs=pl.BlockSpec((tm, D), lambda i: (i, 0)),
        out_shape=jax.ShapeDtypeStruct((n, D), F32),
        scratch_shapes=[pltpu.VMEM((tm, D), BF16), pltpu.VMEM((tm, D), BF16)],
        compiler_params=_params(1),
        name="sgu",
    )(x, mod, g, win, lng, ws, bs_full, wout)


def _softmax_pv(s, sink, v):
    m = jnp.maximum(jnp.max(s, axis=-1, keepdims=True), sink)
    p = jnp.exp(s - m)
    den = jnp.sum(p, axis=-1, keepdims=True) + jnp.exp(sink - m)
    return _dot(p.astype(BF16), v) / den


def _attn_prompt_kernel(x_ref, mod_ref, g_ref, wqkv_ref, wo_ref, sink_ref, o_ref, k_ref, v_ref,
                        q_s, k_s, v_s, o_s, *, bb, seq):
    x = x_ref[...]
    mod = mod_ref[0]
    h = _modulate(x, g_ref[...], mod[0:1], mod[1:2]).astype(BF16)
    nq = N_HEADS * HD
    q_s[...] = (_dot(h, wqkv_ref[:, 0:nq]) * (HD ** -0.5)).astype(BF16)
    k = _dot(h, wqkv_ref[:, nq:nq + KV_DIM])
    v = _dot(h, wqkv_ref[:, nq + KV_DIM:nq + 2 * KV_DIM])
    k_ref[...] = k
    v_ref[...] = v
    k_s[...] = k.astype(BF16)
    v_s[...] = v.astype(BF16)

    def one_batch(b, carry):
        rows = pl.ds(pl.multiple_of(b * seq, seq), seq)
        for kv in range(N_KV):
            kh = k_s[rows, kv * HD:(kv + 1) * HD]
            vh = v_s[rows, kv * HD:(kv + 1) * HD]
            for j in range(Q_PER_KV):
                hd = kv * Q_PER_KV + j
                qh = q_s[rows, hd * HD:(hd + 1) * HD]
                oh = _softmax_pv(_dot_nt(qh, kh), sink_ref[0, hd], vh)
                o_s[rows, hd * HD:(hd + 1) * HD] = oh.astype(BF16)
        return carry

    lax.fori_loop(0, bb, one_batch, 0)
    o_ref[...] = x + mod[2:3] * _dot(o_s[...], wo_ref[...])


def _attn_prompt(x, mod, g, wqkv, wo, sink, *, seq, bb):
    n = x.shape[0]
    tm = bb * seq
    nq = N_HEADS * HD
    return pl.pallas_call(
        functools.partial(_attn_prompt_kernel, bb=bb, seq=seq),
        grid=(n // tm,),
        in_specs=[pl.BlockSpec((tm, D), lambda i: (i, 0)), _mod_spec(n, mod.shape[0], tm),
                  _resident((1, D)), _resident(wqkv.shape), _resident(wo.shape),
                  pl.BlockSpec(memory_space=pltpu.SMEM)],
        out_specs=[pl.BlockSpec((tm, D), lambda i: (i, 0)),
                   pl.BlockSpec((tm, KV_DIM), lambda i: (i, 0)),
                   pl.BlockSpec((tm, KV_DIM), lambda i: (i, 0))],
        out_shape=[jax.ShapeDtypeStruct((n, D), F32),
                   jax.ShapeDtypeStruct((n, KV_DIM), F32),
                   jax.ShapeDtypeStruct((n, KV_DIM), F32)],
        scratch_shapes=[pltpu.VMEM((tm, nq), BF16), pltpu.VMEM((tm, KV_DIM), BF16),
                        pltpu.VMEM((tm, KV_DIM), BF16), pltpu.VMEM((tm, nq), BF16)],
        compiler_params=_params(1),
        name="attn_prompt",
    )(x, mod, g, wqkv, wo, sink)


def _rope_tables(seq):
    t = jnp.arange(seq)
    half = HD // 4
    inv = ROPE_THETA ** (-jnp.arange(half, dtype=F32) / half)
    ang_row = (t // GRID_W).astype(F32)[:, None] * inv[None, :]
    ang_col = (t % GRID_W).astype(F32)[:, None] * inv[None, :]
    cos_h = jnp.concatenate([jnp.cos(ang_row)] * 2 + [jnp.cos(ang_col)] * 2, axis=-1)
    sin_h = jnp.concatenate([-jnp.sin(ang_row), jnp.sin(ang_row),
                             -jnp.sin(ang_col), jnp.sin(ang_col)], axis=-1)
    reps = LANES // HD
    return jnp.tile(cos_h, (1, reps)), jnp.tile(sin_h, (1, reps))


def _qkv_rope_kernel(x_ref, mod_ref, g_ref, wqkv_ref, cos_ref, sin_ref, q_ref, k_ref, v_ref):
    x = x_ref[...]
    mod = mod_ref[0]
    h = _modulate(x, g_ref[...], mod[0:1], mod[1:2]).astype(BF16)
    nq = N_HEADS * HD
    cos = cos_ref[...]
    sin = sin_ref[...]
    lane = lax.broadcasted_iota(jnp.int32, cos.shape, 1)
    low = (lane % (HD // 2)) < (HD // 4)
    for blk in range((nq + KV_DIM) // LANES):
        c0 = blk * LANES
        a = _dot(h, wqkv_ref[:, c0:c0 + LANES])
        partner = jnp.where(low, pltpu.roll(a, LANES - HD // 4, 1), pltpu.roll(a, HD // 4, 1))
        r = a * cos + partner * sin
        if c0 < nq:
            q_ref[:, c0:c0 + LANES] = (r * (HD ** -0.5)).astype(BF16)
        else:
            k_ref[:, c0 - nq:c0 - nq + LANES] = r.astype(BF16)
    v_ref[...] = _dot(h, wqkv_ref[:, nq + KV_DIM:nq + 2 * KV_DIM]).astype(BF16)


def _qkv_rope(x, mod, g, wqkv, cos, sin, *, seq, tm):
    n = x.shape[0]
    nq = N_HEADS * HD
    tiles_per_seq = seq // tm
    return pl.pallas_call(
        _qkv_rope_kernel,
        grid=(n // tm,),
        in_specs=[pl.BlockSpec((tm, D), lambda i: (i, 0)), _mod_spec(n, mod.shape[0], tm),
                  _resident((1, D)), _resident(wqkv.shape),
                  pl.BlockSpec((tm, LANES), lambda i: (i % tiles_per_seq, 0)),
                  pl.BlockSpec((tm, LANES), lambda i: (i % tiles_per_seq, 0))],
        out_specs=[pl.BlockSpec((tm, nq), lambda i: (i, 0)),
                   pl.BlockSpec((tm, KV_DIM), lambda i: (i, 0)),
                   pl.BlockSpec((tm, KV_DIM), lambda i: (i, 0))],
        out_shape=[jax.ShapeDtypeStruct((n, nq), BF16),
                   jax.ShapeDtypeStruct((n, KV_DIM), BF16),
                   jax.ShapeDtypeStruct((n, KV_DIM), BF16)],
        compiler_params=_params(1),
        name="qkv_rope",
    )(x, mod, g, wqkv, cos, sin)


def _attn_latent_kernel(x_ref, mod_ref, q_ref, k_ref, v_ref, ck_ref, cv_ref, wo_ref, sink_ref,
                        o_ref, o_s, *, tq, seq, span):
    qt = pl.program_id(1)
    ws = pl.multiple_of(jnp.clip(qt * tq - WINDOW, 0, seq - span), WINDOW)
    past = ck_ref.shape[1]
    qpos = qt * tq + lax.broadcasted_iota(jnp.int32, (tq, span + past), 0)
    col = lax.broadcasted_iota(jnp.int32, (tq, span + past), 1)
    valid = (jnp.abs(qpos - (ws + col)) <= WINDOW) | (col >= span)

    for kv in range(N_KV):
        lanes = slice(kv * HD, (kv + 1) * HD)
        kcat = jnp.concatenate([k_ref[0, pl.ds(ws, span), lanes], ck_ref[0, :, lanes]], axis=0)
        vcat = jnp.concatenate([v_ref[0, pl.ds(ws, span), lanes], cv_ref[0, :, lanes]], axis=0)
        for j in range(Q_PER_KV):
            hd = kv * Q_PER_KV + j
            s = jnp.where(valid, _dot_nt(q_ref[0, :, hd * HD:(hd + 1) * HD], kcat), NEG_BIG)
            o_s[:, hd * HD:(hd + 1) * HD] = _softmax_pv(s, sink_ref[0, hd], vcat).astype(BF16)

    mod = mod_ref[0]
    o_ref[0] = x_ref[0] + mod[2:3] * _dot(o_s[...], wo_ref[...])


def _attn_latent(x, mod, q, k, v, ck, cv, wo, sink, *, tq):
    nb, seq, _ = x.shape
    nq = N_HEADS * HD
    past = ck.shape[1]
    span = tq + 2 * WINDOW
    return pl.pallas_call(
        functools.partial(_attn_latent_kernel, tq=tq, seq=seq, span=span),
        grid=(nb, seq // tq),
        in_specs=[pl.BlockSpec((1, tq, D), lambda b, t: (b, t, 0)),
                  pl.BlockSpec((1, 6, D), lambda b, t: (b, 0, 0)),
                  pl.BlockSpec((1, tq, nq), lambda b, t: (b, t, 0)),
                  pl.BlockSpec((1, seq, KV_DIM), lambda b, t: (b, 0, 0)),
                  pl.BlockSpec((1, seq, KV_DIM), lambda b, t: (b, 0, 0)),
                  pl.BlockSpec((1, past, KV_DIM), lambda b, t: (b, 0, 0)),
                  pl.BlockSpec((1, past, KV_DIM), lambda b, t: (b, 0, 0)),
                  _resident(wo.shape),
                  pl.BlockSpec(memory_space=pltpu.SMEM)],
        out_specs=pl.BlockSpec((1, tq, D), lambda b, t: (b, t, 0)),
        out_shape=jax.ShapeDtypeStruct((nb, seq, D), F32),
        scratch_shapes=[pltpu.VMEM((tq, nq), BF16)],
        compiler_params=_params(2),
        name="attn_latent",
    )(x, mod, q, k, v, ck, cv, wo, sink)


def _dft_angle(k, t, period):
    return ((k * t) % period).astype(F32) * (2.0 * math.pi / period)


def _fourier_tables(seq):
    c = jnp.arange(FOURIER_GROUP_DIM, dtype=jnp.int32)
    ang_c = _dft_angle(c[:, None], c[None, :], FOURIER_GROUP_DIM)
    chan = jnp.concatenate([jnp.cos(ang_c), jnp.sin(ang_c)], axis=1).astype(BF16)
    k = jnp.arange(seq, dtype=jnp.int32)
    hi = jnp.arange(seq // LANES, dtype=jnp.int32) * LANES
    lo = jnp.arange(LANES, dtype=jnp.int32)
    ang_hi = _dft_angle(k[:, None], hi[None, :], seq)
    ang_lo = _dft_angle(k[:, None], lo[None, :], seq)
    return chan, jnp.cos(ang_hi), jnp.sin(ang_hi), jnp.cos(ang_lo), jnp.sin(ang_lo)


def _fourier_chan_kernel(x_ref, mod_ref, g_ref, chan_ref, o_ref, *, bb, tt):
    x = x_ref[...].reshape(bb * tt, D)
    mod = mod_ref[0]
    h = _modulate(x, g_ref[...], mod[0:1], mod[1:2]).astype(BF16)
    for gi in range(D // FOURIER_GROUP_DIM):
        cols = slice(gi * LANES, (gi + 1) * LANES)
        cs = _dot(h[:, cols], chan_ref[...]).astype(BF16)
        for b in range(bb):
            rows = slice(b * tt, (b + 1) * tt)
            o_ref[b, 0, :, cols] = cs[rows, 0:LANES]
            o_ref[b, 1, :, cols] = cs[rows, LANES:2 * LANES]


def _fourier_chan(x, mod, g, chan, *, bb, tt):
    nb, seq, _ = x.shape
    n_groups = mod.shape[0]
    return pl.pallas_call(
        functools.partial(_fourier_chan_kernel, bb=bb, tt=tt),
        grid=(nb // bb, seq // tt),
        in_specs=[pl.BlockSpec((bb, tt, D), lambda b, t: (b, t, 0)),
                  pl.BlockSpec((1, 6, D), lambda b, t: ((b * bb * n_groups) // nb, 0, 0)),
                  _resident((1, D)), _resident(chan.shape)],
        out_specs=pl.BlockSpec((bb, 2, tt, D), lambda b, t: (b, 0, t, 0)),
        out_shape=jax.ShapeDtypeStruct((nb, 2, seq, D), BF16),
        compiler_params=_params(2),
        name="fourier_chan",
    )(x, mod, g, chan)


def _fourier_pos_kernel(x_ref, mod_ref, hh_ref, chi_ref, shi_ref, clo_ref, slo_ref, wout_ref,
                        o_ref, lhs_ref, *, seq, scale):
    clo = clo_ref[...]
    slo = slo_ref[...]
    for a in range(seq // LANES):
        ca = chi_ref[:, a:a + 1]
        sa = shi_ref[:, a:a + 1]
        lhs_ref[:, a * LANES:(a + 1) * LANES] = (ca * clo - sa * slo).astype(BF16)
        lhs_ref[:, seq + a * LANES:seq + (a + 1) * LANES] = (-(sa * clo + ca * slo)).astype(BF16)
    f = (_dot(lhs_ref[...], hh_ref[0]) * scale).astype(BF16)
    mod = mod_ref[0]
    o_ref[0] = x_ref[0] + mod[2:3] * _dot(f, wout_ref[...])


def _fourier_pos(x, mod, hh, chi, shi, clo, slo, wout, *, tr):
    nb, seq, _ = x.shape
    n_groups = mod.shape[0]
    n_hi = seq // LANES
    scale = 1.0 / math.sqrt(seq * FOURIER_GROUP_DIM)
    return pl.pallas_call(
        functools.partial(_fourier_pos_kernel, seq=seq, scale=scale),
        grid=(nb, seq // tr),
        in_specs=[pl.BlockSpec((1, tr, D), lambda b, t: (b, t, 0)),
                  pl.BlockSpec((1, 6, D), lambda b, t: ((b * n_groups) // nb, 0, 0)),
                  pl.BlockSpec((1, 2 * seq, D), lambda b, t: (b, 0, 0)),
                  pl.BlockSpec((tr, n_hi), lambda b, t: (t, 0)),
                  pl.BlockSpec((tr, n_hi), lambda b, t: (t, 0)),
                  pl.BlockSpec((tr, LANES), lambda b, t: (t, 0)),
                  pl.BlockSpec((tr, LANES), lambda b, t: (t, 0)),
                  _resident(wout.shape)],
        out_specs=pl.BlockSpec((1, tr, D), lambda b, t: (b, t, 0)),
        out_shape=jax.ShapeDtypeStruct((nb, seq, D), F32),
        scratch_shapes=[pltpu.VMEM((tr, 2 * seq), BF16)],
        compiler_params=_params(2),
        name="fourier_pos",
    )(x, mod, hh, chi, shi, clo, slo, wout)


def _fourier(x, mod, g, wout, *, bb, tt, tr):
    nb, seq, _ = x.shape
    chan, chi, shi, clo, slo = _fourier_tables(seq)
    hh = _fourier_chan(x, mod, g, chan, bb=bb, tt=tt)
    return _fourier_pos(x, mod, hh.reshape(nb, 2 * seq, D), chi, shi, clo, slo, wout, tr=tr)


def _fourier_short_kernel(x_ref, mod_ref, g_ref, chan_ref, chi_ref, shi_ref, clo_ref, slo_ref,
                          wout_ref, o_ref, lhs_ref, hh_ref, f_ref, *, bb, seq, scale):
    @pl.when(pl.program_id(0) == 0)
    def _():
        clo = clo_ref[...]
        slo = slo_ref[...]
        for a in range(seq // LANES):
            ca = chi_ref[:, a:a + 1]
            sa = shi_ref[:, a:a + 1]
            lhs_ref[:, a * LANES:(a + 1) * LANES] = (ca * clo - sa * slo).astype(BF16)
            lhs_ref[:, seq + a * LANES:seq + (a + 1) * LANES] = (-(sa * clo + ca * slo)).astype(BF16)

    x = x_ref[...]
    mod = mod_ref[0]
    h = _modulate(x, g_ref[...], mod[0:1], mod[1:2]).astype(BF16)
    for gi in range(D // FOURIER_GROUP_DIM):
        cols = slice(gi * LANES, (gi + 1) * LANES)
        cs = _dot(h[:, cols], chan_ref[...]).astype(BF16)
        for b in range(bb):
            rows = slice(b * seq, (b + 1) * seq)
            hh_ref[b, 0:seq, cols] = cs[rows, 0:LANES]
            hh_ref[b, seq:2 * seq, cols] = cs[rows, LANES:2 * LANES]
    for b in range(bb):
        f_ref[b * seq:(b + 1) * seq, :] = (_dot(lhs_ref[...], hh_ref[b]) * scale).astype(BF16)
    o_ref[...] = x + mod[2:3] * _dot(f_ref[...], wout_ref[...])


def _fourier_short(x, mod, g, wout, *, seq, bb):
    n = x.shape[0]
    tm = bb * seq
    chan, chi, shi, clo, slo = _fourier_tables(seq)
    scale = 1.0 / math.sqrt(seq * FOURIER_GROUP_DIM)
    return pl.pallas_call(
        functools.partial(_fourier_short_kernel, bb=bb, seq=seq, scale=scale),
        grid=(n // tm,),
        in_specs=[pl.BlockSpec((tm, D), lambda i: (i, 0)), _mod_spec(n, mod.shape[0], tm),
                  _resident((1, D)), _resident(chan.shape), _resident(chi.shape),
                  _resident(shi.shape), _resident(clo.shape), _resident(slo.shape),
                  _resident(wout.shape)],
        out_specs=pl.BlockSpec((tm, D), lambda i: (i, 0)),
        out_shape=jax.ShapeDtypeStruct((n, D), F32),
        scratch_shapes=[pltpu.VMEM((seq, 2 * seq), BF16), pltpu.VMEM((bb, 2 * seq, D), BF16),
                        pltpu.VMEM((tm, D), BF16)],
        compiler_params=_params(1),
        name="fourier_short",
    )(x, mod, g, chan, chi, shi, clo, slo, wout)


def kernel(x_prompt, x_sample, cache_k, cache_v, c, c_ctx, ada_w, ada_b, norm_mix_g, norm_ffn_g,
           final_g, attn_wqkv, attn_wo, attn_sink, sgu_w_in, sgu_ln_g, sgu_w_s, sgu_b_s,
           sgu_w_out, sc_w_in, sc_conv, sc_w_out, fn_w_out, ffn_w_up, ffn_conv, ffn_w_down):
    nbp, seq_p, _ = x_prompt.shape
    nbs, seq_s, _ = x_sample.shape
    depth = ada_w.shape[0]
    tm = 512

    cvec = jnp.concatenate([c_ctx[None, :], c, jnp.zeros((8 - 1 - nbs, D), F32)], axis=0)
    mods = _ada_mod(cvec, ada_w, ada_b).reshape(depth, 8, 6, D)

    xp = x_prompt.reshape(nbp * seq_p, D)
    xs = x_sample.reshape(nbs * seq_s, D)
    new_k = new_v = None
    wup = ffn_w_up.astype(BF16)
    wdn = ffn_w_down.astype(BF16)
    g_ffn = norm_ffn_g[:, None, :]

    for l in range(depth):
        kind, j = l % 4, l // 4
        mp = mods[l, 0:1]
        ms = mods[l, 1:1 + nbs]
        g_mix = norm_mix_g[l][None, :]
        if kind == 0:
            wqkv = attn_wqkv[j].astype(BF16)
            wo = attn_wo[j].astype(BF16)
            sink = attn_sink[j][None, :]
            xp, new_k, new_v = _attn_prompt(xp, mp, g_mix, wqkv, wo, sink, seq=seq_p, bb=2)
            cos, sin = _rope_tables(seq_s)
            q, k, v = _qkv_rope(xs, ms, g_mix, wqkv, cos, sin, seq=seq_s, tm=tm)
            past = cache_k.shape[2]
            ck = cache_k[:, j].reshape(nbs, past, KV_DIM).astype(BF16)
            cv = cache_v[:, j].reshape(nbs, past, KV_DIM).astype(BF16)
            xs = _attn_latent(xs.reshape(nbs, seq_s, D), ms, q.reshape(nbs, seq_s, N_HEADS * HD),
                              k.reshape(nbs, seq_s, KV_DIM), v.reshape(nbs, seq_s, KV_DIM),
                              ck, cv, wo, sink, tq=256).reshape(nbs * seq_s, D)
        elif kind == 1:
            win = sgu_w_in[j].astype(BF16)
            ws = sgu_w_s[j].astype(BF16)
            wout = sgu_w_out[j].astype(BF16)
            lng = sgu_ln_g[j][None, :]
            bs_full = jnp.repeat(sgu_b_s[j].T, LANES, axis=1)
            xp = _sgu(xp, mp, g_mix, win, lng, ws, bs_full, wout, tm=tm)
            xs = _sgu(xs, ms, g_mix, win, lng, ws, bs_full, wout, tm=tm)
        elif kind == 2:
            win = sc_w_in[j].astype(BF16)
            wout = sc_w_out[j].astype(BF16)
            xp = _sconv(xp, mp, g_mix, win, sc_conv[j], wout, seq=seq_p, tm=tm)
            xs = _sconv(xs, ms, g_mix, win, sc_conv[j], wout, seq=seq_s, tm=tm)
        else:
            wout = fn_w_out[j].astype(BF16)
            xp = _fourier_short(xp, mp, g_mix, wout, seq=seq_p, bb=4)
            xs = _fourier(xs.reshape(nbs, seq_s, D), ms, g_mix, wout,
                          bb=1, tt=tm, tr=tm).reshape(nbs * seq_s, D)

        fg = final_g[None, :] if l == depth - 1 else None
        xp = _ffn(xp, mp, g_ffn, wup, ffn_conv, wdn, fg, layer=l, seq=seq_p, tm=2 * tm)
        xs = _ffn(xs, ms, g_ffn, wup, ffn_conv, wdn, fg, layer=l, seq=seq_s, tm=2 * tm)

    y_prompt = xp.reshape(nbp, seq_p, D)
    y_sample = xs.reshape(nbs, seq_s, D)
    new_cache_k = new_k.reshape(nbp, 1, seq_p, N_KV, HD)
    new_cache_v = new_v.reshape(nbp, 1, seq_p, N_KV, HD)
    return (y_prompt, y_sample, new_cache_k, new_cache_v)
```

```python
import functools
import math

import jax
import jax.numpy as jnp
from jax import lax
from jax.experimental import pallas as pl
from jax.experimental.pallas import tpu as pltpu

D = 1024
N_HEADS = 16
N_KV = 4
HD = 64
Q_PER_KV = N_HEADS // N_KV
KV_DIM = N_KV * HD
WINDOW = 128
GRID_W = 64
ROPE_THETA = 10000.0
SGU_CHUNK = 128
SGU_GROUPS = 8
FOURIER_GROUP_DIM = 128
D_FF = 2816
EPS = 1e-6
NEG_BIG = -1e30

LANES = 128
HALO_ROWS = 8
FF_CHUNK = 256
UNIT = 256
VMEM_LIMIT = 52 * 1024 * 1024

F32 = jnp.float32
BF16 = jnp.bfloat16


def _resident(shape):
    nd = len(shape)
    return pl.BlockSpec(shape, lambda *_: (0,) * nd, pipeline_mode=pl.Buffered(1))


def _params(n_axes, flags=None):
    return pltpu.CompilerParams(dimension_semantics=("arbitrary",) * n_axes,
                                vmem_limit_bytes=VMEM_LIMIT, flags=flags)


def _modulate(x, g, shift, scale):
    ms = jnp.mean(x * x, axis=-1, keepdims=True)
    return (x * lax.rsqrt(ms + EPS)) * (g * (1.0 + scale)) + shift


def _rms(x, g):
    ms = jnp.mean(x * x, axis=-1, keepdims=True)
    return (x * lax.rsqrt(ms + EPS)) * g


def _dot(a, b):
    return jnp.dot(a, b, preferred_element_type=F32)


def _dot_nt(a, b):
    return lax.dot_general(a, b, (((1,), (1,)), ((), ())), preferred_element_type=F32)


def _ada_kernel(c_ref, w_ref, b_ref, o_ref):
    c = c_ref[...]
    s = (c * jax.nn.sigmoid(c)).astype(BF16)
    o_ref[0] = _dot(s, w_ref[0].astype(BF16)) + b_ref[0]


def _ada_mod(cvec, ada_w, ada_b):
    depth, _, n6 = ada_w.shape
    bn = D
    return pl.pallas_call(
        _ada_kernel,
        grid=(depth, n6 // bn),
        in_specs=[pl.BlockSpec((8, D), lambda l, j: (0, 0)),
                  pl.BlockSpec((1, D, bn), lambda l, j: (l, 0, j)),
                  pl.BlockSpec((1, 1, bn), lambda l, j: (l, 0, j))],
        out_specs=pl.BlockSpec((1, 8, bn), lambda l, j: (l, 0, j)),
        out_shape=jax.ShapeDtypeStruct((depth, 8, n6), F32),
        compiler_params=_params(2),
        name="ada_mod",
    )(cvec, ada_w, ada_b.reshape(depth, 1, n6))


def _halo_specs(n_rows, tm):
    t8 = tm // HALO_ROWS
    last = n_rows // HALO_ROWS - 1
    return [pl.BlockSpec((HALO_ROWS, D), lambda i: (jnp.maximum(i * t8 - 1, 0), 0)),
            pl.BlockSpec((HALO_ROWS, D), lambda i: (jnp.minimum((i + 1) * t8, last), 0))]


def _mod_spec(n_rows, n_groups, tm):
    rows_per_group = n_rows // n_groups
    return pl.BlockSpec((1, 6, D), lambda i: ((i * tm) // rows_per_group, 0, 0))


class _Units:
    def __init__(self, n_rows, seq, tm):
        assert n_rows % tm == 0 and tm % UNIT == 0
        self.tm, self.seq = tm, seq
        self.n = tm // UNIT
        self.halo = seq > UNIT
        assert seq == UNIT or (seq > tm and seq % tm == 0)
        self.ext_h = UNIT + 2 * HALO_ROWS if self.halo else UNIT
        self.ext_a = UNIT + 2 * HALO_ROWS

    def fill_hext(self, hext_ref, x, xp_ref, xn_ref, g, shift, scale, tile_idx):
        h = _modulate(x, g, shift, scale)
        for u in range(self.n):
            hext_ref[u, 0:UNIT, :] = h[u * UNIT:(u + 1) * UNIT].astype(BF16)
        if not self.halo:
            return
        tiles_per_seq = self.seq // self.tm
        at_start = (tile_idx % tiles_per_seq) == 0
        at_end = (tile_idx % tiles_per_seq) == tiles_per_seq - 1
        for u in range(self.n):
            if u > 0:
                prev = h[u * UNIT - HALO_ROWS:u * UNIT]
            else:
                prev = jnp.where(at_start, 0.0, _modulate(xp_ref[...], g, shift, scale))
            if u < self.n - 1:
                nxt = h[(u + 1) * UNIT:(u + 1) * UNIT + HALO_ROWS]
            else:
                nxt = jnp.where(at_end, 0.0, _modulate(xn_ref[...], g, shift, scale))
            hext_ref[u, UNIT:UNIT + HALO_ROWS, :] = prev.astype(BF16)
            hext_ref[u, UNIT + HALO_ROWS:UNIT + 2 * HALO_ROWS, :] = nxt.astype(BF16)


    def zero_gaps(self, ref):
        zeros = jnp.zeros((HALO_ROWS, LANES), ref.dtype)
        for c in range(ref.shape[0]):
            ref[c, 0:HALO_ROWS, :] = zeros
            ref[c, HALO_ROWS + UNIT:2 * HALO_ROWS + UNIT, :] = zeros

    def store_padded(self, ref, r):
        for c in range(ref.shape[0]):
            cols = slice(c * LANES, (c + 1) * LANES)
            ref[c, HALO_ROWS:HALO_ROWS + UNIT, :] = r[0:UNIT, cols]
            if self.halo:
                ref[c, 0:HALO_ROWS, :] = r[UNIT:UNIT + HALO_ROWS, cols]
                ref[c, HALO_ROWS + UNIT:2 * HALO_ROWS + UNIT, :] = (
                    r[UNIT + HALO_ROWS:UNIT + 2 * HALO_ROWS, cols])

    def conv3(self, ref, w):
        b = HALO_ROWS
        out = []
        for c in range(ref.shape[0]):
            wc = w[:, c * LANES:(c + 1) * LANES]
            out.append(ref[c, b - 1:b - 1 + UNIT, :] * wc[0:1] + ref[c, b:b + UNIT, :] * wc[1:2]
                       + ref[c, b + 1:b + 1 + UNIT, :] * wc[2:3])
        return jnp.concatenate(out, axis=1)


def _ffn_kernel(*refs, units, final):
    halo, n_units = units.halo, units.n
    it = iter(refs)
    x_ref = next(it)
    xp_ref = next(it) if halo else None
    xn_ref = next(it) if halo else None
    mod_ref, g_ref, wup_ref, cw_ref, wdn_ref = (next(it) for _ in range(5))
    fg_ref = next(it) if final else None
    o_ref, hext_ref, acc_ref = (next(it) for _ in range(3))
    a_refs = [[[next(it), next(it)] for _ in range(n_units)] for _ in range(2)]

    x = x_ref[...]
    mod = mod_ref[0]
    units.fill_hext(hext_ref, x, xp_ref, xn_ref, g_ref[...], mod[3:4], mod[4:5], pl.program_id(0))
    if not halo:
        for p in range(2):
            for u in range(n_units):
                for t in range(2):
                    units.zero_gaps(a_refs[p][u][t])

    def up(u, j):
        c0 = j * FF_CHUNK
        hx = hext_ref[u]
        for t, col in enumerate((c0, D_FF + c0)):
            units.store_padded(a_refs[j % 2][u][t], _dot(hx, wup_ref[:, col:col + FF_CHUNK]))

    n_chunks = D_FF // FF_CHUNK
    for u in range(n_units):
        up(u, 0)
    for j in range(n_chunks):
        c0 = j * FF_CHUNK
        p = j % 2
        for u in range(n_units):
            if j + 1 < n_chunks:
                up(u, j + 1)
            cg = units.conv3(a_refs[p][u][0], cw_ref[:, c0:c0 + FF_CHUNK])
            cu = units.conv3(a_refs[p][u][1], cw_ref[:, D_FF + c0:D_FF + c0 + FF_CHUNK])
            act = ((cg * jax.nn.sigmoid(cg)) * cu).astype(BF16)
            part = _dot(act, wdn_ref[c0:c0 + FF_CHUNK, :])
            rows = slice(u * UNIT, (u + 1) * UNIT)
            if j == 0:
                acc_ref[rows, :] = part
            else:
                acc_ref[rows, :] += part

    y = x + mod[5:6] * acc_ref[...]
    if final:
        y = _rms(y, fg_ref[...])
    o_ref[...] = y


def _layer_resident(arr, layer):
    nd = arr.ndim - 1
    return pl.BlockSpec((None,) + arr.shape[1:], lambda *_: (layer,) + (0,) * nd,
                        pipeline_mode=pl.Buffered(1))


def _ffn(x, mod, g, wup, cw, wdn, final_g, *, layer, seq, tm):
    n = x.shape[0]
    units = _Units(n, seq, tm)
    halo = units.halo
    final = final_g is not None
    in_specs = [pl.BlockSpec((tm, D), lambda i: (i, 0))]
    args = [x]
    if halo:
        in_specs += _halo_specs(n, tm)
        args += [x, x]
    in_specs += [_mod_spec(n, mod.shape[0], tm), _layer_resident(g, layer),
                 _layer_resident(wup, layer), _layer_resident(cw, layer),
                 _layer_resident(wdn, layer)]
    args += [mod, g, wup, cw, wdn]
    if final:
        in_specs.append(_resident((1, D)))
        args.append(final_g)
    scratch = [pltpu.VMEM((units.n, units.ext_h, D), BF16), pltpu.VMEM((tm, D), F32)]
    scratch += [pltpu.VMEM((FF_CHUNK // LANES, units.ext_a, LANES), F32)] * (4 * units.n)
    return pl.pallas_call(
        functools.partial(_ffn_kernel, units=units, final=final),
        grid=(n // tm,),
        in_specs=in_specs,
        out_specs=pl.BlockSpec((tm, D), lambda i: (i, 0)),
        out_shape=jax.ShapeDtypeStruct((n, D), F32),
        scratch_shapes=scratch,
        compiler_params=_params(1),
        name="conv_ffn",
    )(*args)


def _sconv_kernel(*refs, units):
    halo, n_units = units.halo, units.n
    it = iter(refs)
    x_ref = next(it)
    xp_ref = next(it) if halo else None
    xn_ref = next(it) if halo else None
    mod_ref, g_ref, win_ref, cw_ref, wout_ref, o_ref, hext_ref, t_ref = (next(it) for _ in range(8))
    p_refs = [[next(it) for _ in range(n_units)] for _ in range(2)]

    x = x_ref[...]
    mod = mod_ref[0]
    units.fill_hext(hext_ref, x, xp_ref, xn_ref, g_ref[...], mod[0:1], mod[1:2], pl.program_id(0))
    if not halo:
        for p in range(2):
            for u in range(n_units):
                units.zero_gaps(p_refs[p][u])

    for j in range(D // FF_CHUNK):
        c0 = j * FF_CHUNK
        for u in range(n_units):
            hx = hext_ref[u]
            b = _dot(hx, win_ref[:, c0:c0 + FF_CHUNK])
            cg = _dot(hx, win_ref[:, D + c0:D + c0 + FF_CHUNK])
            xin = _dot(hx, win_ref[:, 2 * D + c0:2 * D + c0 + FF_CHUNK])
            units.store_padded(p_refs[j % 2][u], cg * xin)
            y = units.conv3(p_refs[j % 2][u], cw_ref[:, c0:c0 + FF_CHUNK])
            t_ref[u * UNIT:(u + 1) * UNIT, c0:c0 + FF_CHUNK] = (b[0:UNIT] * y).astype(BF16)

    o_ref[...] = x + mod[2:3] * _dot(t_ref[...], wout_ref[...])


def _sconv(x, mod, g, win, cw, wout, *, seq, tm):
    n = x.shape[0]
    units = _Units(n, seq, tm)
    halo = units.halo
    in_specs = [pl.BlockSpec((tm, D), lambda i: (i, 0))]
    args = [x]
    if halo:
        in_specs += _halo_specs(n, tm)
        args += [x, x]
    in_specs += [_mod_spec(n, mod.shape[0], tm), _resident((1, D)), _resident(win.shape),
                 _resident(cw.shape), _resident(wout.shape)]
    args += [mod, g, win, cw, wout]
    scratch = [pltpu.VMEM((units.n, units.ext_h, D), BF16), pltpu.VMEM((tm, D), BF16)]
    scratch += [pltpu.VMEM((FF_CHUNK // LANES, units.ext_a, LANES), F32)] * (2 * units.n)
    return pl.pallas_call(
        functools.partial(_sconv_kernel, units=units),
        grid=(n // tm,),
        in_specs=in_specs,
        out_specs=pl.BlockSpec((tm, D), lambda i: (i, 0)),
        out_shape=jax.ShapeDtypeStruct((n, D), F32),
        scratch_shapes=scratch,
        compiler_params=_params(1),
        name="short_conv",
    )(*args)


def _gelu_tanh(x):
    return 0.5 * x * (1.0 + jnp.tanh(math.sqrt(2.0 / math.pi) * (x + 0.044715 * (x * x * x))))


def _sgu_kernel(x_ref, mod_ref, g_ref, win_ref, lng_ref, ws_ref, bs_ref, wout_ref, o_ref,
                v_ref, t_ref, *, tm):
    x = x_ref[...]
    mod = mod_ref[0]
    h = _modulate(x, g_ref[...], mod[0:1], mod[1:2]).astype(BF16)
    v = _gelu_tanh(_dot(h, win_ref[:, D:2 * D]))
    mu = jnp.mean(v, axis=-1, keepdims=True)
    vc = v - mu
    var = jnp.mean(vc * vc, axis=-1, keepdims=True)
    v_ref[...] = ((vc * lax.rsqrt(var + EPS)) * lng_ref[...]).astype(BF16)

    n_chunks = tm // SGU_CHUNK
    for gi in range(SGU_GROUPS):
        cols = slice(gi * LANES, (gi + 1) * LANES)
        if gi % 2 == 0:
            u_pair = _gelu_tanh(_dot(h, win_ref[:, gi * LANES:(gi + 2) * LANES]))
        u = u_pair[:, (gi % 2) * LANES:(gi % 2 + 1) * LANES]
        rhs = jnp.concatenate(
            [v_ref[n * SGU_CHUNK:(n + 1) * SGU_CHUNK, cols] for n in range(n_chunks)], axis=1)
        mixed = _dot(ws_ref[gi], rhs)
        bias = bs_ref[:, cols]
        for n in range(n_chunks):
            rows = slice(n * SGU_CHUNK, (n + 1) * SGU_CHUNK)
            t_ref[rows, cols] = (u[rows] * (mixed[:, n * LANES:(n + 1) * LANES] + bias)).astype(BF16)

    o_ref[...] = x + mod[2:3] * _dot(t_ref[...], wout_ref[...])


def _sgu(x, mod, g, win, lng, ws, bs_full, wout, *, tm):
    n = x.shape[0]
    return pl.pallas_call(
        functools.partial(_sgu_kernel, tm=tm),
        grid=(n // tm,),
        in_specs=[pl.BlockSpec((tm, D), lambda i: (i, 0)), _mod_spec(n, mod.shape[0], tm),
                  _resident((1, D)), _resident(win.shape), _resident((1, D)), _resident(ws.shape),
                  _resident(bs_full.shape), _resident(wout.shape)],
        out_specs=pl.BlockSpec((tm, D), lambda i: (i, 0)),
        out_shape=jax.ShapeDtypeStruct((n, D), F32),
        scratch_shapes=[pltpu.VMEM((tm, D), BF16), pltpu.VMEM((tm, D), BF16)],
        compiler_params=_params(1),
        name="sgu",
    )(x, mod, g, win, lng, ws, bs_full, wout)


def _softmax_pv(s, sink, v):
    m = jnp.maximum(jnp.max(s, axis=-1, keepdims=True), sink)
    p = jnp.exp(s - m)
    den = jnp.sum(p, axis=-1, keepdims=True) + jnp.exp(sink - m)
    return _dot(p.astype(BF16), v) / den


def _attn_prompt_kernel(x_ref, mod_ref, g_ref, wqkv_ref, wo_ref, sink_ref, o_ref, k_ref, v_ref,
                        q_s, k_s, v_s, o_s, *, bb, seq):
    x = x_ref[...]
    mod = mod_ref[0]
    h = _modulate(x, g_ref[...], mod[0:1], mod[1:2]).astype(BF16)
    nq = N_HEADS * HD
    q_s[...] = (_dot(h, wqkv_ref[:, 0:nq]) * (HD ** -0.5)).astype(BF16)
    k = _dot(h, wqkv_ref[:, nq:nq + KV_DIM])
    v = _dot(h, wqkv_ref[:, nq + KV_DIM:nq + 2 * KV_DIM])
    k_ref[...] = k
    v_ref[...] = v
    k_s[...] = k.astype(BF16)
    v_s[...] = v.astype(BF16)

    def one_batch(b, carry):
        rows = pl.ds(pl.multiple_of(b * seq, seq), seq)
        for kv in range(N_KV):
            kh = k_s[rows, kv * HD:(kv + 1) * HD]
            vh = v_s[rows, kv * HD:(kv + 1) * HD]
            scores = [_dot_nt(q_s[rows, hd * HD:(hd + 1) * HD], kh)
                      for hd in range(kv * Q_PER_KV, (kv + 1) * Q_PER_KV)]
            for j, s in enumerate(scores):
                hd = kv * Q_PER_KV + j
                oh = _softmax_pv(s, sink_ref[0, hd], vh)
                o_s[rows, hd * HD:(hd + 1) * HD] = oh.astype(BF16)
        return carry

    lax.fori_loop(0, bb, one_batch, 0)
    o_ref[...] = x + mod[2:3] * _dot(o_s[...], wo_ref[...])


def _attn_prompt(x, mod, g, wqkv, wo, sink, *, seq, bb):
    n = x.shape[0]
    tm = bb * seq
    nq = N_HEADS * HD
    return pl.pallas_call(
        functools.partial(_attn_prompt_kernel, bb=bb, seq=seq),
        grid=(n // tm,),
        in_specs=[pl.BlockSpec((tm, D), lambda i: (i, 0)), _mod_spec(n, mod.shape[0], tm),
                  _resident((1, D)), _resident(wqkv.shape), _resident(wo.shape),
                  pl.BlockSpec(memory_space=pltpu.SMEM)],
        out_specs=[pl.BlockSpec((tm, D), lambda i: (i, 0)),
                   pl.BlockSpec((tm, KV_DIM), lambda i: (i, 0)),
                   pl.BlockSpec((tm, KV_DIM), lambda i: (i, 0))],
        out_shape=[jax.ShapeDtypeStruct((n, D), F32),
                   jax.ShapeDtypeStruct((n, KV_DIM), F32),
                   jax.ShapeDtypeStruct((n, KV_DIM), F32)],
        scratch_shapes=[pltpu.VMEM((tm, nq), BF16), pltpu.VMEM((tm, KV_DIM), BF16),
                        pltpu.VMEM((tm, KV_DIM), BF16), pltpu.VMEM((tm, nq), BF16)],
        compiler_params=_params(1),
        name="attn_prompt",
    )(x, mod, g, wqkv, wo, sink)


def _rope_tables(seq):
    t = jnp.arange(seq)
    half = HD // 4
    inv = ROPE_THETA ** (-jnp.arange(half, dtype=F32) / half)
    ang_row = (t // GRID_W).astype(F32)[:, None] * inv[None, :]
    ang_col = (t % GRID_W).astype(F32)[:, None] * inv[None, :]
    cos_h = jnp.concatenate([jnp.cos(ang_row)] * 2 + [jnp.cos(ang_col)] * 2, axis=-1)
    sin_h = jnp.concatenate([-jnp.sin(ang_row), jnp.sin(ang_row),
                             -jnp.sin(ang_col), jnp.sin(ang_col)], axis=-1)
    reps = LANES // HD
    return jnp.tile(cos_h, (1, reps)), jnp.tile(sin_h, (1, reps))


def _qkv_rope_kernel(x_ref, mod_ref, g_ref, wqkv_ref, cos_ref, sin_ref, q_ref, k_ref, v_ref):
    x = x_ref[...]
    mod = mod_ref[0]
    h = _modulate(x, g_ref[...], mod[0:1], mod[1:2]).astype(BF16)
    nq = N_HEADS * HD
    cos = cos_ref[...]
    sin = sin_ref[...]
    lane = lax.broadcasted_iota(jnp.int32, cos.shape, 1)
    low = (lane % (HD // 2)) < (HD // 4)
    for blk in range((nq + KV_DIM) // LANES):
        c0 = blk * LANES
        a = _dot(h, wqkv_ref[:, c0:c0 + LANES])
        partner = jnp.where(low, pltpu.roll(a, LANES - HD // 4, 1), pltpu.roll(a, HD // 4, 1))
        r = a * cos + partner * sin
        if c0 < nq:
            q_ref[:, c0:c0 + LANES] = (r * (HD ** -0.5)).astype(BF16)
        else:
            k_ref[:, c0 - nq:c0 - nq + LANES] = r.astype(BF16)
    v_ref[...] = _dot(h, wqkv_ref[:, nq + KV_DIM:nq + 2 * KV_DIM]).astype(BF16)


def _qkv_rope(x, mod, g, wqkv, cos, sin, *, seq, tm):
    n = x.shape[0]
    nq = N_HEADS * HD
    tiles_per_seq = seq // tm
    return pl.pallas_call(
        _qkv_rope_kernel,
        grid=(n // tm,),
        in_specs=[pl.BlockSpec((tm, D), lambda i: (i, 0)), _mod_spec(n, mod.shape[0], tm),
                  _resident((1, D)), _resident(wqkv.shape),
                  pl.BlockSpec((tm, LANES), lambda i: (i % tiles_per_seq, 0)),
                  pl.BlockSpec((tm, LANES), lambda i: (i % tiles_per_seq, 0))],
        out_specs=[pl.BlockSpec((tm, nq), lambda i: (i, 0)),
                   pl.BlockSpec((tm, KV_DIM), lambda i: (i, 0)),
                   pl.BlockSpec((tm, KV_DIM), lambda i: (i, 0))],
        out_shape=[jax.ShapeDtypeStruct((n, nq), BF16),
                   jax.ShapeDtypeStruct((n, KV_DIM), BF16),
                   jax.ShapeDtypeStruct((n, KV_DIM), BF16)],
        compiler_params=_params(1),
        name="qkv_rope",
    )(x, mod, g, wqkv, cos, sin)


def _attn_latent_kernel(x_ref, mod_ref, q_ref, k_ref, v_ref, ck_ref, cv_ref, wo_ref, sink_ref,
                        o_ref, o_s, *, tq, seq, span):
    qt = pl.program_id(1)
    ws = pl.multiple_of(jnp.clip(qt * tq - WINDOW, 0, seq - span), WINDOW)
    past = ck_ref.shape[1]
    qpos = qt * tq + lax.broadcasted_iota(jnp.int32, (tq, span + past), 0)
    col = lax.broadcasted_iota(jnp.int32, (tq, span + past), 1)
    valid = (jnp.abs(qpos - (ws + col)) <= WINDOW) | (col >= span)

    for kv in range(N_KV):
        lanes = slice(kv * HD, (kv + 1) * HD)
        kcat = jnp.concatenate([k_ref[0, pl.ds(ws, span), lanes], ck_ref[0, :, lanes]], axis=0)
        vcat = jnp.concatenate([v_ref[0, pl.ds(ws, span), lanes], cv_ref[0, :, lanes]], axis=0)
        scores = [_dot_nt(q_ref[0, :, hd * HD:(hd + 1) * HD], kcat)
                  for hd in range(kv * Q_PER_KV, (kv + 1) * Q_PER_KV)]
        for j, s in enumerate(scores):
            hd = kv * Q_PER_KV + j
            s = jnp.where(valid, s, NEG_BIG)
            o_s[:, hd * HD:(hd + 1) * HD] = _softmax_pv(s, sink_ref[0, hd], vcat).astype(BF16)

    mod = mod_ref[0]
    o_ref[0] = x_ref[0] + mod[2:3] * _dot(o_s[...], wo_ref[...])


def _attn_latent(x, mod, q, k, v, ck, cv, wo, sink, *, tq):
    nb, seq, _ = x.shape
    nq = N_HEADS * HD
    past = ck.shape[1]
    span = tq + 2 * WINDOW
    return pl.pallas_call(
        functools.partial(_attn_latent_kernel, tq=tq, seq=seq, span=span),
        grid=(nb, seq // tq),
        in_specs=[pl.BlockSpec((1, tq, D), lambda b, t: (b, t, 0)),
                  pl.BlockSpec((1, 6, D), lambda b, t: (b, 0, 0)),
                  pl.BlockSpec((1, tq, nq), lambda b, t: (b, t, 0)),
                  pl.BlockSpec((1, seq, KV_DIM), lambda b, t: (b, 0, 0)),
                  pl.BlockSpec((1, seq, KV_DIM), lambda b, t: (b, 0, 0)),
                  pl.BlockSpec((1, past, KV_DIM), lambda b, t: (b, 0, 0)),
                  pl.BlockSpec((1, past, KV_DIM), lambda b, t: (b, 0, 0)),
                  _resident(wo.shape),
                  pl.BlockSpec(memory_space=pltpu.SMEM)],
        out_specs=pl.BlockSpec((1, tq, D), lambda b, t: (b, t, 0)),
        out_shape=jax.ShapeDtypeStruct((nb, seq, D), F32),
        scratch_shapes=[pltpu.VMEM((tq, nq), BF16)],
        compiler_params=_params(2),
        name="attn_latent",
    )(x, mod, q, k, v, ck, cv, wo, sink)


def _dft_angle(k, t, period):
    return ((k * t) % period).astype(F32) * (2.0 * math.pi / period)


def _fourier_tables(seq):
    c = jnp.arange(FOURIER_GROUP_DIM, dtype=jnp.int32)
    ang_c = _dft_angle(c[:, None], c[None, :], FOURIER_GROUP_DIM)
    chan = jnp.concatenate([jnp.cos(ang_c), jnp.sin(ang_c)], axis=1).astype(BF16)
    k = jnp.arange(seq, dtype=jnp.int32)
    hi = jnp.arange(seq // LANES, dtype=jnp.int32) * LANES
    lo = jnp.arange(LANES, dtype=jnp.int32)
    ang_hi = _dft_angle(k[:, None], hi[None, :], seq)
    ang_lo = _dft_angle(k[:, None], lo[None, :], seq)
    return chan, jnp.cos(ang_hi), jnp.sin(ang_hi), jnp.cos(ang_lo), jnp.sin(ang_lo)


def _fourier_chan_kernel(x_ref, mod_ref, g_ref, chan_ref, o_ref, *, bb, tt):
    x = x_ref[...].reshape(bb * tt, D)
    mod = mod_ref[0]
    h = _modulate(x, g_ref[...], mod[0:1], mod[1:2]).astype(BF16)
    for gi in range(D // FOURIER_GROUP_DIM):
        cols = slice(gi * LANES, (gi + 1) * LANES)
        cs = _dot(h[:, cols], chan_ref[...]).astype(BF16)
        for b in range(bb):
            rows = slice(b * tt, (b + 1) * tt)
            o_ref[b, 0, :, cols] = cs[rows, 0:LANES]
            o_ref[b, 1, :, cols] = cs[rows, LANES:2 * LANES]


def _fourier_chan(x, mod, g, chan, *, bb, tt):
    nb, seq, _ = x.shape
    n_groups = mod.shape[0]
    return pl.pallas_call(
        functools.partial(_fourier_chan_kernel, bb=bb, tt=tt),
        grid=(nb // bb, seq // tt),
        in_specs=[pl.BlockSpec((bb, tt, D), lambda b, t: (b, t, 0)),
                  pl.BlockSpec((1, 6, D), lambda b, t: ((b * bb * n_groups) // nb, 0, 0)),
                  _resident((1, D)), _resident(chan.shape)],
        out_specs=pl.BlockSpec((bb, 2, tt, D), lambda b, t: (b, 0, t, 0)),
        out_shape=jax.ShapeDtypeStruct((nb, 2, seq, D), BF16),
        compiler_params=_params(2),
        name="fourier_chan",
    )(x, mod, g, chan)


def _fourier_pos_kernel(x_ref, mod_ref, hh_ref, chi_ref, shi_ref, clo_ref, slo_ref, wout_ref,
                        o_ref, lhs_ref, *, seq, scale):
    clo = clo_ref[...]
    slo = slo_ref[...]
    for a in range(seq // LANES):
        ca = chi_ref[:, a:a + 1]
        sa = shi_ref[:, a:a + 1]
        lhs_ref[:, a * LANES:(a + 1) * LANES] = (ca * clo - sa * slo).astype(BF16)
        lhs_ref[:, seq + a * LANES:seq + (a + 1) * LANES] = (-(sa * clo + ca * slo)).astype(BF16)
    f = (_dot(lhs_ref[...], hh_ref[0]) * scale).astype(BF16)
    mod = mod_ref[0]
    o_ref[0] = x_ref[0] + mod[2:3] * _dot(f, wout_ref[...])


def _fourier_pos(x, mod, hh, chi, shi, clo, slo, wout, *, tr):
    nb, seq, _ = x.shape
    n_groups = mod.shape[0]
    n_hi = seq // LANES
    scale = 1.0 / math.sqrt(seq * FOURIER_GROUP_DIM)
    return pl.pallas_call(
        functools.partial(_fourier_pos_kernel, seq=seq, scale=scale),
        grid=(nb, seq // tr),
        in_specs=[pl.BlockSpec((1, tr, D), lambda b, t: (b, t, 0)),
                  pl.BlockSpec((1, 6, D), lambda b, t: ((b * n_groups) // nb, 0, 0)),
                  pl.BlockSpec((1, 2 * seq, D), lambda b, t: (b, 0, 0)),
                  pl.BlockSpec((tr, n_hi), lambda b, t: (t, 0)),
                  pl.BlockSpec((tr, n_hi), lambda b, t: (t, 0)),
                  pl.BlockSpec((tr, LANES), lambda b, t: (t, 0)),
                  pl.BlockSpec((tr, LANES), lambda b, t: (t, 0)),
                  _resident(wout.shape)],
        out_specs=pl.BlockSpec((1, tr, D), lambda b, t: (b, t, 0)),
        out_shape=jax.ShapeDtypeStruct((nb, seq, D), F32),
        scratch_shapes=[pltpu.VMEM((tr, 2 * seq), BF16)],
        compiler_params=_params(2),
        name="fourier_pos",
    )(x, mod, hh, chi, shi, clo, slo, wout)


def _fourier(x, mod, g, wout, *, bb, tt, tr):
    nb, seq, _ = x.shape
    chan, chi, shi, clo, slo = _fourier_tables(seq)
    hh = _fourier_chan(x, mod, g, chan, bb=bb, tt=tt)
    return _fourier_pos(x, mod, hh.reshape(nb, 2 * seq, D), chi, shi, clo, slo, wout, tr=tr)


def _fourier_short_kernel(x_ref, mod_ref, g_ref, chan_ref, chi_ref, shi_ref, clo_ref, slo_ref,
                          wout_ref, o_ref, lhs_ref, hh_ref, f_ref, *, bb, seq, scale):
    @pl.when(pl.program_id(0) == 0)
    def _():
        clo = clo_ref[...]
        slo = slo_ref[...]
        for a in range(seq // LANES):
            ca = chi_ref[:, a:a + 1]
            sa = shi_ref[:, a:a + 1]
            lhs_ref[:, a * LANES:(a + 1) * LANES] = (ca * clo - sa * slo).astype(BF16)
            lhs_ref[:, seq + a * LANES:seq + (a + 1) * LANES] = (-(sa * clo + ca * slo)).astype(BF16)

    x = x_ref[...]
    mod = mod_ref[0]
    h = _modulate(x, g_ref[...], mod[0:1], mod[1:2]).astype(BF16)
    for gi in range(D // FOURIER_GROUP_DIM):
        cols = slice(gi * LANES, (gi + 1) * LANES)
        cs = _dot(h[:, cols], chan_ref[...]).astype(BF16)
        for b in range(bb):
            rows = slice(b * seq, (b + 1) * seq)
            hh_ref[b, 0:seq, cols] = cs[rows, 0:LANES]
            hh_ref[b, seq:2 * seq, cols] = cs[rows, LANES:2 * LANES]
    for b in range(bb):
        f_ref[b * seq:(b + 1) * seq, :] = (_dot(lhs_ref[...], hh_ref[b]) * scale).astype(BF16)
    o_ref[...] = x + mod[2:3] * _dot(f_ref[...], wout_ref[...])


def _fourier_short(x, mod, g, wout, *, seq, bb):
    n = x.shape[0]
    tm = bb * seq
    chan, chi, shi, clo, slo = _fourier_tables(seq)
    scale = 1.0 / math.sqrt(seq * FOURIER_GROUP_DIM)
    return pl.pallas_call(
        functools.partial(_fourier_short_kernel, bb=bb, seq=seq, scale=scale),
        grid=(n // tm,),
        in_specs=[pl.BlockSpec((tm, D), lambda i: (i, 0)), _mod_spec(n, mod.shape[0], tm),
                  _resident((1, D)), _resident(chan.shape), _resident(chi.shape),
                  _resident(shi.shape), _resident(clo.shape), _resident(slo.shape),
                  _resident(wout.shape)],
        out_specs=pl.BlockSpec((tm, D), lambda i: (i, 0)),
        out_shape=jax.ShapeDtypeStruct((n, D), F32),
        scratch_shapes=[pltpu.VMEM((seq, 2 * seq), BF16), pltpu.VMEM((bb, 2 * seq, D), BF16),
                        pltpu.VMEM((tm, D), BF16)],
        compiler_params=_params(1),
        name="fourier_short",
    )(x, mod, g, chan, chi, shi, clo, slo, wout)


def kernel(x_prompt, x_sample, cache_k, cache_v, c, c_ctx, ada_w, ada_b, norm_mix_g, norm_ffn_g,
           final_g, attn_wqkv, attn_wo, attn_sink, sgu_w_in, sgu_ln_g, sgu_w_s, sgu_b_s,
           sgu_w_out, sc_w_in, sc_conv, sc_w_out, fn_w_out, ffn_w_up, ffn_conv, ffn_w_down):
    nbp, seq_p, _ = x_prompt.shape
    nbs, seq_s, _ = x_sample.shape
    depth = ada_w.shape[0]
    tm = 512

    cvec = jnp.concatenate([c_ctx[None, :], c, jnp.zeros((8 - 1 - nbs, D), F32)], axis=0)
    mods = _ada_mod(cvec, ada_w, ada_b).reshape(depth, 8, 6, D)

    xp = x_prompt.reshape(nbp * seq_p, D)
    xs = x_sample.reshape(nbs * seq_s, D)
    new_k = new_v = None
    wup = ffn_w_up.astype(BF16)
    wdn = ffn_w_down.astype(BF16)
    g_ffn = norm_ffn_g[:, None, :]

    for l in range(depth):
        kind, j = l % 4, l // 4
        mp = mods[l, 0:1]
        ms = mods[l, 1:1 + nbs]
        g_mix = norm_mix_g[l][None, :]
        if kind == 0:
            wqkv = attn_wqkv[j].astype(BF16)
            wo = attn_wo[j].astype(BF16)
            sink = attn_sink[j][None, :]
            xp, new_k, new_v = _attn_prompt(xp, mp, g_mix, wqkv, wo, sink, seq=seq_p, bb=2)
            cos, sin = _rope_tables(seq_s)
            q, k, v = _qkv_rope(xs, ms, g_mix, wqkv, cos, sin, seq=seq_s, tm=tm)
            past = cache_k.shape[2]
            ck = cache_k[:, j].reshape(nbs, past, KV_DIM).astype(BF16)
            cv = cache_v[:, j].reshape(nbs, past, KV_DIM).astype(BF16)
            xs = _attn_latent(xs.reshape(nbs, seq_s, D), ms, q.reshape(nbs, seq_s, N_HEADS * HD),
                              k.reshape(nbs, seq_s, KV_DIM), v.reshape(nbs, seq_s, KV_DIM),
                              ck, cv, wo, sink, tq=256).reshape(nbs * seq_s, D)
        elif kind == 1:
            win = sgu_w_in[j].astype(BF16)
            ws = sgu_w_s[j].astype(BF16)
            wout = sgu_w_out[j].astype(BF16)
            lng = sgu_ln_g[j][None, :]
            bs_full = jnp.repeat(sgu_b_s[j].T, LANES, axis=1)
            xp = _sgu(xp, mp, g_mix, win, lng, ws, bs_full, wout, tm=tm)
            xs = _sgu(xs, ms, g_mix, win, lng, ws, bs_full, wout, tm=tm)
        elif kind == 2:
            win = sc_w_in[j].astype(BF16)
            wout = sc_w_out[j].astype(BF16)
            xp = _sconv(xp, mp, g_mix, win, sc_conv[j], wout, seq=seq_p, tm=tm)
            xs = _sconv(xs, ms, g_mix, win, sc_conv[j], wout, seq=seq_s, tm=tm)
        else:
            wout = fn_w_out[j].astype(BF16)
            xp = _fourier_short(xp, mp, g_mix, wout, seq=seq_p, bb=4)
            xs = _fourier(xs.reshape(nbs, seq_s, D), ms, g_mix, wout,
                          bb=1, tt=tm, tr=tm).reshape(nbs * seq_s, D)

        fg = final_g[None, :] if l == depth - 1 else None
        xp = _ffn(xp, mp, g_ffn, wup, ffn_conv, wdn, fg, layer=l, seq=seq_p, tm=tm)
        xs = _ffn(xs, ms, g_ffn, wup, ffn_conv, wdn, fg, layer=l, seq=seq_s, tm=tm)

    y_prompt = xp.reshape(nbp, seq_p, D)
    y_sample = xs.reshape(nbs, seq_s, D)
    new_cache_k = new_k.reshape(nbp, 1, seq_p, N_KV, HD)
    new_cache_v = new_v.reshape(nbp, 1, seq_p, N_KV, HD)
    return (y_prompt, y_sample, new_cache_k, new_cache_v)
```

```python
import functools
import math

import jax
import jax.numpy as jnp
from jax import lax
from jax.experimental import pallas as pl
from jax.experimental.pallas import tpu as pltpu

D = 1024
N_HEADS = 16
N_KV = 4
HD = 64
Q_PER_KV = N_HEADS // N_KV
KV_DIM = N_KV * HD
WINDOW = 128
GRID_W = 64
ROPE_THETA = 10000.0
SGU_CHUNK = 128
SGU_GROUPS = 8
FOURIER_GROUP_DIM = 128
D_FF = 2816
EPS = 1e-6
NEG_BIG = -1e30

LANES = 128
HALO_ROWS = 8
FF_CHUNK = 256
UNIT = 256
VMEM_LIMIT = 52 * 1024 * 1024

F32 = jnp.float32
BF16 = jnp.bfloat16


def _resident(shape):
    nd = len(shape)
    return pl.BlockSpec(shape, lambda *_: (0,) * nd, pipeline_mode=pl.Buffered(1))


def _params(n_axes, flags=None):
    return pltpu.CompilerParams(dimension_semantics=("arbitrary",) * n_axes,
                                vmem_limit_bytes=VMEM_LIMIT, flags=flags)


def _modulate(x, g, shift, scale):
    ms = jnp.mean(x * x, axis=-1, keepdims=True)
    return (x * lax.rsqrt(ms + EPS)) * (g * (1.0 + scale)) + shift


def _rms(x, g):
    ms = jnp.mean(x * x, axis=-1, keepdims=True)
    return (x * lax.rsqrt(ms + EPS)) * g


def _dot(a, b):
    return jnp.dot(a, b, preferred_element_type=F32)


def _dot_nt(a, b):
    return lax.dot_general(a, b, (((1,), (1,)), ((), ())), preferred_element_type=F32)


def _ada_kernel(c_ref, w_ref, b_ref, o_ref):
    c = c_ref[...]
    s = (c * jax.nn.sigmoid(c)).astype(BF16)
    o_ref[0] = _dot(s, w_ref[0].astype(BF16)) + b_ref[0]


def _ada_mod(cvec, ada_w, ada_b):
    depth, _, n6 = ada_w.shape
    bn = D
    return pl.pallas_call(
        _ada_kernel,
        grid=(depth, n6 // bn),
        in_specs=[pl.BlockSpec((8, D), lambda l, j: (0, 0)),
                  pl.BlockSpec((1, D, bn), lambda l, j: (l, 0, j)),
                  pl.BlockSpec((1, 1, bn), lambda l, j: (l, 0, j))],
        out_specs=pl.BlockSpec((1, 8, bn), lambda l, j: (l, 0, j)),
        out_shape=jax.ShapeDtypeStruct((depth, 8, n6), F32),
        compiler_params=_params(2),
        name="ada_mod",
    )(cvec, ada_w, ada_b.reshape(depth, 1, n6))


def _halo_specs(n_rows, tm):
    t8 = tm // HALO_ROWS
    last = n_rows // HALO_ROWS - 1
    return [pl.BlockSpec((HALO_ROWS, D), lambda i: (jnp.maximum(i * t8 - 1, 0), 0)),
            pl.BlockSpec((HALO_ROWS, D), lambda i: (jnp.minimum((i + 1) * t8, last), 0))]


def _mod_spec(n_rows, n_groups, tm):
    rows_per_group = n_rows // n_groups
    return pl.BlockSpec((1, 6, D), lambda i: ((i * tm) // rows_per_group, 0, 0))


class _Units:
    def __init__(self, n_rows, seq, tm):
        assert n_rows % tm == 0 and tm % UNIT == 0
        self.tm, self.seq = tm, seq
        self.n = tm // UNIT
        self.halo = seq > UNIT
        assert seq == UNIT or (seq > tm and seq % tm == 0)
        self.ext_h = UNIT + 2 * HALO_ROWS if self.halo else UNIT
        self.ext_a = UNIT + 2 * HALO_ROWS

    def fill_hext(self, hext_ref, x, xp_ref, xn_ref, g, shift, scale, tile_idx):
        h = _modulate(x, g, shift, scale)
        for u in range(self.n):
            hext_ref[u, 0:UNIT, :] = h[u * UNIT:(u + 1) * UNIT].astype(BF16)
        if not self.halo:
            return
        tiles_per_seq = self.seq // self.tm
        at_start = (tile_idx % tiles_per_seq) == 0
        at_end = (tile_idx % tiles_per_seq) == tiles_per_seq - 1
        for u in range(self.n):
            if u > 0:
                prev = h[u * UNIT - HALO_ROWS:u * UNIT]
            else:
                prev = jnp.where(at_start, 0.0, _modulate(xp_ref[...], g, shift, scale))
            if u < self.n - 1:
                nxt = h[(u + 1) * UNIT:(u + 1) * UNIT + HALO_ROWS]
            else:
                nxt = jnp.where(at_end, 0.0, _modulate(xn_ref[...], g, shift, scale))
            hext_ref[u, UNIT:UNIT + HALO_ROWS, :] = prev.astype(BF16)
            hext_ref[u, UNIT + HALO_ROWS:UNIT + 2 * HALO_ROWS, :] = nxt.astype(BF16)


    def zero_gaps(self, ref):
        zeros = jnp.zeros((HALO_ROWS, LANES), ref.dtype)
        for c in range(ref.shape[0]):
            ref[c, 0:HALO_ROWS, :] = zeros
            ref[c, HALO_ROWS + UNIT:2 * HALO_ROWS + UNIT, :] = zeros

    def store_padded(self, ref, r):
        for c in range(ref.shape[0]):
            cols = slice(c * LANES, (c + 1) * LANES)
            ref[c, HALO_ROWS:HALO_ROWS + UNIT, :] = r[0:UNIT, cols]
            if self.halo:
                ref[c, 0:HALO_ROWS, :] = r[UNIT:UNIT + HALO_ROWS, cols]
                ref[c, HALO_ROWS + UNIT:2 * HALO_ROWS + UNIT, :] = (
                    r[UNIT + HALO_ROWS:UNIT + 2 * HALO_ROWS, cols])

    def conv3(self, ref, w):
        b = HALO_ROWS
        out = []
        for c in range(ref.shape[0]):
            wc = w[:, c * LANES:(c + 1) * LANES]
            out.append(ref[c, b - 1:b - 1 + UNIT, :] * wc[0:1] + ref[c, b:b + UNIT, :] * wc[1:2]
                       + ref[c, b + 1:b + 1 + UNIT, :] * wc[2:3])
        return jnp.concatenate(out, axis=1)


def _ffn_kernel(*refs, units, final):
    halo, n_units = units.halo, units.n
    it = iter(refs)
    x_ref = next(it)
    xp_ref = next(it) if halo else None
    xn_ref = next(it) if halo else None
    mod_ref, g_ref, wup_ref, cw_ref, wdn_ref = (next(it) for _ in range(5))
    fg_ref = next(it) if final else None
    o_ref, hext_ref, acc_ref = (next(it) for _ in range(3))
    a_refs = [[[next(it), next(it)] for _ in range(n_units)] for _ in range(2)]

    x = x_ref[...]
    mod = mod_ref[0]
    units.fill_hext(hext_ref, x, xp_ref, xn_ref, g_ref[...], mod[3:4], mod[4:5], pl.program_id(0))
    if not halo:
        for p in range(2):
            for u in range(n_units):
                for t in range(2):
                    units.zero_gaps(a_refs[p][u][t])

    def up(u, j):
        c0 = j * FF_CHUNK
        hx = hext_ref[u]
        for t, col in enumerate((c0, D_FF + c0)):
            units.store_padded(a_refs[j % 2][u][t], _dot(hx, wup_ref[:, col:col + FF_CHUNK]))

    n_chunks = D_FF // FF_CHUNK
    for u in range(n_units):
        up(u, 0)
    for j in range(n_chunks):
        c0 = j * FF_CHUNK
        p = j % 2
        for u in range(n_units):
            if j + 1 < n_chunks:
                up(u, j + 1)
            cg = units.conv3(a_refs[p][u][0], cw_ref[:, c0:c0 + FF_CHUNK])
            cu = units.conv3(a_refs[p][u][1], cw_ref[:, D_FF + c0:D_FF + c0 + FF_CHUNK])
            act = ((cg * jax.nn.sigmoid(cg)) * cu).astype(BF16)
            part = _dot(act, wdn_ref[c0:c0 + FF_CHUNK, :])
            rows = slice(u * UNIT, (u + 1) * UNIT)
            if j == 0:
                acc_ref[rows, :] = part
            else:
                acc_ref[rows, :] += part

    y = x + mod[5:6] * acc_ref[...]
    if final:
        y = _rms(y, fg_ref[...])
    o_ref[...] = y


def _layer_resident(arr, layer):
    nd = arr.ndim - 1
    return pl.BlockSpec((None,) + arr.shape[1:], lambda *_: (layer,) + (0,) * nd,
                        pipeline_mode=pl.Buffered(1))


def _ffn(x, mod, g, wup, cw, wdn, final_g, *, layer, seq, tm):
    n = x.shape[0]
    units = _Units(n, seq, tm)
    halo = units.halo
    final = final_g is not None
    in_specs = [pl.BlockSpec((tm, D), lambda i: (i, 0))]
    args = [x]
    if halo:
        in_specs += _halo_specs(n, tm)
        args += [x, x]
    in_specs += [_mod_spec(n, mod.shape[0], tm), _layer_resident(g, layer),
                 _layer_resident(wup, layer), _layer_resident(cw, layer),
                 _layer_resident(wdn, layer)]
    args += [mod, g, wup, cw, wdn]
    if final:
        in_specs.append(_resident((1, D)))
        args.append(final_g)
    scratch = [pltpu.VMEM((units.n, units.ext_h, D), BF16), pltpu.VMEM((tm, D), F32)]
    scratch += [pltpu.VMEM((FF_CHUNK // LANES, units.ext_a, LANES), F32)] * (4 * units.n)
    return pl.pallas_call(
        functools.partial(_ffn_kernel, units=units, final=final),
        grid=(n // tm,),
        in_specs=in_specs,
        out_specs=pl.BlockSpec((tm, D), lambda i: (i, 0)),
        out_shape=jax.ShapeDtypeStruct((n, D), F32),
        scratch_shapes=scratch,
        compiler_params=_params(1),
        name="conv_ffn",
    )(*args)


def _sconv_kernel(*refs, units):
    halo, n_units = units.halo, units.n
    it = iter(refs)
    x_ref = next(it)
    xp_ref = next(it) if halo else None
    xn_ref = next(it) if halo else None
    mod_ref, g_ref, win_ref, cw_ref, wout_ref, o_ref, hext_ref, t_ref = (next(it) for _ in range(8))
    p_refs = [[next(it) for _ in range(n_units)] for _ in range(2)]

    x = x_ref[...]
    mod = mod_ref[0]
    units.fill_hext(hext_ref, x, xp_ref, xn_ref, g_ref[...], mod[0:1], mod[1:2], pl.program_id(0))
    if not halo:
        for p in range(2):
            for u in range(n_units):
                units.zero_gaps(p_refs[p][u])

    for j in range(D // FF_CHUNK):
        c0 = j * FF_CHUNK
        for u in range(n_units):
            hx = hext_ref[u]
            b = _dot(hx, win_ref[:, c0:c0 + FF_CHUNK])
            cg = _dot(hx, win_ref[:, D + c0:D + c0 + FF_CHUNK])
            xin = _dot(hx, win_ref[:, 2 * D + c0:2 * D + c0 + FF_CHUNK])
            units.store_padded(p_refs[j % 2][u], cg * xin)
            y = units.conv3(p_refs[j % 2][u], cw_ref[:, c0:c0 + FF_CHUNK])
            t_ref[u * UNIT:(u + 1) * UNIT, c0:c0 + FF_CHUNK] = (b[0:UNIT] * y).astype(BF16)

    o_ref[...] = x + mod[2:3] * _dot(t_ref[...], wout_ref[...])


def _sconv(x, mod, g, win, cw, wout, *, seq, tm):
    n = x.shape[0]
    units = _Units(n, seq, tm)
    halo = units.halo
    in_specs = [pl.BlockSpec((tm, D), lambda i: (i, 0))]
    args = [x]
    if halo:
        in_specs += _halo_specs(n, tm)
        args += [x, x]
    in_specs += [_mod_spec(n, mod.shape[0], tm), _resident((1, D)), _resident(win.shape),
                 _resident(cw.shape), _resident(wout.shape)]
    args += [mod, g, win, cw, wout]
    scratch = [pltpu.VMEM((units.n, units.ext_h, D), BF16), pltpu.VMEM((tm, D), BF16)]
    scratch += [pltpu.VMEM((FF_CHUNK // LANES, units.ext_a, LANES), F32)] * (2 * units.n)
    return pl.pallas_call(
        functools.partial(_sconv_kernel, units=units),
        grid=(n // tm,),
        in_specs=in_specs,
        out_specs=pl.BlockSpec((tm, D), lambda i: (i, 0)),
        out_shape=jax.ShapeDtypeStruct((n, D), F32),
        scratch_shapes=scratch,
        compiler_params=_params(1),
        name="short_conv",
    )(*args)


def _gelu_tanh(x):
    return 0.5 * x * (1.0 + jnp.tanh(math.sqrt(2.0 / math.pi) * (x + 0.044715 * (x * x * x))))


def _sgu_kernel(x_ref, mod_ref, g_ref, win_ref, lng_ref, ws_ref, bs_ref, wout_ref, o_ref,
                v_ref, t_ref, *, tm):
    x = x_ref[...]
    mod = mod_ref[0]
    h = _modulate(x, g_ref[...], mod[0:1], mod[1:2]).astype(BF16)
    v = _gelu_tanh(_dot(h, win_ref[:, D:2 * D]))
    mu = jnp.mean(v, axis=-1, keepdims=True)
    vc = v - mu
    var = jnp.mean(vc * vc, axis=-1, keepdims=True)
    v_ref[...] = ((vc * lax.rsqrt(var + EPS)) * lng_ref[...]).astype(BF16)

    n_chunks = tm // SGU_CHUNK
    for gi in range(SGU_GROUPS):
        cols = slice(gi * LANES, (gi + 1) * LANES)
        if gi % 2 == 0:
            u_pair = _gelu_tanh(_dot(h, win_ref[:, gi * LANES:(gi + 2) * LANES]))
        u = u_pair[:, (gi % 2) * LANES:(gi % 2 + 1) * LANES]
        rhs = jnp.concatenate(
            [v_ref[n * SGU_CHUNK:(n + 1) * SGU_CHUNK, cols] for n in range(n_chunks)], axis=1)
        mixed = _dot(ws_ref[gi], rhs)
        bias = bs_ref[:, cols]
        for n in range(n_chunks):
            rows = slice(n * SGU_CHUNK, (n + 1) * SGU_CHUNK)
            t_ref[rows, cols] = (u[rows] * (mixed[:, n * LANES:(n + 1) * LANES] + bias)).astype(BF16)

    o_ref[...] = x + mod[2:3] * _dot(t_ref[...], wout_ref[...])


def _sgu(x, mod, g, win, lng, ws, bs_full, wout, *, tm):
    n = x.shape[0]
    return pl.pallas_call(
        functools.partial(_sgu_kernel, tm=tm),
        grid=(n // tm,),
        in_specs=[pl.BlockSpec((tm, D), lambda i: (i, 0)), _mod_spec(n, mod.shape[0], tm),
                  _resident((1, D)), _resident(win.shape), _resident((1, D)), _resident(ws.shape),
                  _resident(bs_full.shape), _resident(wout.shape)],
        out_specs=pl.BlockSpec((tm, D), lambda i: (i, 0)),
        out_shape=jax.ShapeDtypeStruct((n, D), F32),
        scratch_shapes=[pltpu.VMEM((tm, D), BF16), pltpu.VMEM((tm, D), BF16)],
        compiler_params=_params(1),
        name="sgu",
    )(x, mod, g, win, lng, ws, bs_full, wout)


LOG2E = math.log2(math.e)
Q_SCALE = HD ** -0.5 * LOG2E


def _softmax_pv_t(parts, sink, v_t):
    sink2 = sink * LOG2E
    m = sink2
    for s_t in parts:
        m = jnp.maximum(m, jnp.max(s_t, axis=0, keepdims=True))
    den = jnp.exp2(sink2 - m)
    ps = []
    for s_t in parts:
        p = jnp.exp2(s_t - m)
        den = den + jnp.sum(p, axis=0, keepdims=True)
        ps.append(p.astype(BF16))
    p_all = ps[0] if len(ps) == 1 else jnp.concatenate(ps, axis=0)
    return _dot(v_t, p_all) / den


def _attn_prompt_kernel(x_ref, mod_ref, g_ref, wqkv_ref, wo_ref, sink_ref, o_ref, k_ref, v_ref,
                        q_s, k_s, ot_s, o_s, *, bb, seq):
    x = x_ref[...]
    mod = mod_ref[0]
    h = _modulate(x, g_ref[...], mod[0:1], mod[1:2]).astype(BF16)
    nq = N_HEADS * HD
    q_s[...] = (_dot(h, wqkv_ref[:, 0:nq]) * Q_SCALE).astype(BF16)
    k = _dot(h, wqkv_ref[:, nq:nq + KV_DIM])
    v = _dot(h, wqkv_ref[:, nq + KV_DIM:nq + 2 * KV_DIM])
    k_ref[...] = k
    v_ref[...] = v
    k_s[...] = k.astype(BF16)

    def one_batch(b, carry):
        rows = pl.ds(pl.multiple_of(b * seq, seq), seq)
        v_t = v_ref[rows, :].T.astype(BF16)
        for kv in range(N_KV):
            kh = k_s[rows, kv * HD:(kv + 1) * HD]
            vh_t = v_t[kv * HD:(kv + 1) * HD, :]
            scores = [_dot_nt(kh, q_s[rows, hd * HD:(hd + 1) * HD])
                      for hd in range(kv * Q_PER_KV, (kv + 1) * Q_PER_KV)]
            for j, s_t in enumerate(scores):
                hd = kv * Q_PER_KV + j
                ot_s[hd * HD:(hd + 1) * HD, :] = _softmax_pv_t([s_t], sink_ref[0, hd], vh_t)
        o_s[rows, :] = ot_s[...].T.astype(BF16)
        return carry

    lax.fori_loop(0, bb, one_batch, 0)
    o_ref[...] = x + mod[2:3] * _dot(o_s[...], wo_ref[...])


def _attn_prompt(x, mod, g, wqkv, wo, sink, *, seq, bb):
    n = x.shape[0]
    tm = bb * seq
    nq = N_HEADS * HD
    return pl.pallas_call(
        functools.partial(_attn_prompt_kernel, bb=bb, seq=seq),
        grid=(n // tm,),
        in_specs=[pl.BlockSpec((tm, D), lambda i: (i, 0)), _mod_spec(n, mod.shape[0], tm),
                  _resident((1, D)), _resident(wqkv.shape), _resident(wo.shape),
                  pl.BlockSpec(memory_space=pltpu.SMEM)],
        out_specs=[pl.BlockSpec((tm, D), lambda i: (i, 0)),
                   pl.BlockSpec((tm, KV_DIM), lambda i: (i, 0)),
                   pl.BlockSpec((tm, KV_DIM), lambda i: (i, 0))],
        out_shape=[jax.ShapeDtypeStruct((n, D), F32),
                   jax.ShapeDtypeStruct((n, KV_DIM), F32),
                   jax.ShapeDtypeStruct((n, KV_DIM), F32)],
        scratch_shapes=[pltpu.VMEM((tm, nq), BF16), pltpu.VMEM((tm, KV_DIM), BF16),
                        pltpu.VMEM((nq, seq), F32), pltpu.VMEM((tm, nq), BF16)],
        compiler_params=_params(1),
        name="attn_prompt",
    )(x, mod, g, wqkv, wo, sink)


def _rope_tables(seq):
    t = jnp.arange(seq)
    half = HD // 4
    inv = ROPE_THETA ** (-jnp.arange(half, dtype=F32) / half)
    ang_row = (t // GRID_W).astype(F32)[:, None] * inv[None, :]
    ang_col = (t % GRID_W).astype(F32)[:, None] * inv[None, :]
    cos_h = jnp.concatenate([jnp.cos(ang_row)] * 2 + [jnp.cos(ang_col)] * 2, axis=-1)
    sin_h = jnp.concatenate([-jnp.sin(ang_row), jnp.sin(ang_row),
                             -jnp.sin(ang_col), jnp.sin(ang_col)], axis=-1)
    reps = LANES // HD
    return jnp.tile(cos_h, (1, reps)), jnp.tile(sin_h, (1, reps))


def _qkv_rope_kernel(x_ref, mod_ref, g_ref, wqkv_ref, cos_ref, sin_ref, q_ref, k_ref, v_ref):
    x = x_ref[...]
    mod = mod_ref[0]
    h = _modulate(x, g_ref[...], mod[0:1], mod[1:2]).astype(BF16)
    nq = N_HEADS * HD
    cos = cos_ref[...]
    sin = sin_ref[...]
    lane = lax.broadcasted_iota(jnp.int32, cos.shape, 1)
    low = (lane % (HD // 2)) < (HD // 4)
    for blk in range((nq + KV_DIM) // LANES):
        c0 = blk * LANES
        a = _dot(h, wqkv_ref[:, c0:c0 + LANES])
        partner = jnp.where(low, pltpu.roll(a, LANES - HD // 4, 1), pltpu.roll(a, HD // 4, 1))
        r = a * cos + partner * sin
        if c0 < nq:
            q_ref[:, c0:c0 + LANES] = (r * Q_SCALE).astype(BF16)
        else:
            k_ref[:, c0 - nq:c0 - nq + LANES] = r.astype(BF16)
    v_ref[...] = _dot(h, wqkv_ref[:, nq + KV_DIM:nq + 2 * KV_DIM]).astype(BF16)


def _qkv_rope(x, mod, g, wqkv, cos, sin, *, seq, tm):
    n = x.shape[0]
    nq = N_HEADS * HD
    tiles_per_seq = seq // tm
    return pl.pallas_call(
        _qkv_rope_kernel,
        grid=(n // tm,),
        in_specs=[pl.BlockSpec((tm, D), lambda i: (i, 0)), _mod_spec(n, mod.shape[0], tm),
                  _resident((1, D)), _resident(wqkv.shape),
                  pl.BlockSpec((tm, LANES), lambda i: (i % tiles_per_seq, 0)),
                  pl.BlockSpec((tm, LANES), lambda i: (i % tiles_per_seq, 0))],
        out_specs=[pl.BlockSpec((tm, nq), lambda i: (i, 0)),
                   pl.BlockSpec((tm, KV_DIM), lambda i: (i, 0)),
                   pl.BlockSpec((tm, KV_DIM), lambda i: (i, 0))],
        out_shape=[jax.ShapeDtypeStruct((n, nq), BF16),
                   jax.ShapeDtypeStruct((n, KV_DIM), BF16),
                   jax.ShapeDtypeStruct((n, KV_DIM), BF16)],
        compiler_params=_params(1),
        name="qkv_rope",
    )(x, mod, g, wqkv, cos, sin)


def _attn_latent_kernel(x_ref, mod_ref, q_ref, k_ref, v_ref, ck_ref, cv_ref, wo_ref, sink_ref,
                        o_ref, ot_s, *, tq, seq, span):
    qt = pl.program_id(1)
    ws = pl.multiple_of(jnp.clip(qt * tq - WINDOW, 0, seq - span), WINDOW)
    past = ck_ref.shape[1]
    kpos = ws + lax.broadcasted_iota(jnp.int32, (span, tq), 0)
    qpos = qt * tq + lax.broadcasted_iota(jnp.int32, (span, tq), 1)
    valid = jnp.abs(qpos - kpos) <= WINDOW
    v_t = jnp.concatenate([v_ref[0, pl.ds(ws, span), :].astype(F32).T,
                           cv_ref[0].astype(F32).T], axis=1).astype(BF16)

    for kv in range(N_KV):
        lanes = slice(kv * HD, (kv + 1) * HD)
        kcat = jnp.concatenate([k_ref[0, pl.ds(ws, span), lanes], ck_ref[0, :, lanes]], axis=0)
        scores = [_dot_nt(kcat, q_ref[0, :, hd * HD:(hd + 1) * HD])
                  for hd in range(kv * Q_PER_KV, (kv + 1) * Q_PER_KV)]
        for j, s_t in enumerate(scores):
            hd = kv * Q_PER_KV + j
            parts = [jnp.where(valid, s_t[0:span], NEG_BIG), s_t[span:span + past]]
            ot_s[hd * HD:(hd + 1) * HD, :] = _softmax_pv_t(parts, sink_ref[0, hd], v_t[lanes, :])

    mod = mod_ref[0]
    o_ref[0] = x_ref[0] + mod[2:3] * _dot(ot_s[...].T.astype(BF16), wo_ref[...])


def _attn_latent(x, mod, q, k, v, ck, cv, wo, sink, *, tq):
    nb, seq, _ = x.shape
    nq = N_HEADS * HD
    past = ck.shape[1]
    span = tq + 2 * WINDOW
    return pl.pallas_call(
        functools.partial(_attn_latent_kernel, tq=tq, seq=seq, span=span),
        grid=(nb, seq // tq),
        in_specs=[pl.BlockSpec((1, tq, D), lambda b, t: (b, t, 0)),
                  pl.BlockSpec((1, 6, D), lambda b, t: (b, 0, 0)),
                  pl.BlockSpec((1, tq, nq), lambda b, t: (b, t, 0)),
                  pl.BlockSpec((1, seq, KV_DIM), lambda b, t: (b, 0, 0)),
                  pl.BlockSpec((1, seq, KV_DIM), lambda b, t: (b, 0, 0)),
                  pl.BlockSpec((1, past, KV_DIM), lambda b, t: (b, 0, 0)),
                  pl.BlockSpec((1, past, KV_DIM), lambda b, t: (b, 0, 0)),
                  _resident(wo.shape),
                  pl.BlockSpec(memory_space=pltpu.SMEM)],
        out_specs=pl.BlockSpec((1, tq, D), lambda b, t: (b, t, 0)),
        out_shape=jax.ShapeDtypeStruct((nb, seq, D), F32),
        scratch_shapes=[pltpu.VMEM((nq, tq), F32)],
        compiler_params=_params(2),
        name="attn_latent",
    )(x, mod, q, k, v, ck, cv, wo, sink)


def _dft_angle(k, t, period):
    return ((k * t) % period).astype(F32) * (2.0 * math.pi / period)


def _fourier_tables(seq):
    c = jnp.arange(FOURIER_GROUP_DIM, dtype=jnp.int32)
    ang_c = _dft_angle(c[:, None], c[None, :], FOURIER_GROUP_DIM)
    chan = jnp.concatenate([jnp.cos(ang_c), jnp.sin(ang_c)], axis=1).astype(BF16)
    k = jnp.arange(seq, dtype=jnp.int32)
    hi = jnp.arange(seq // LANES, dtype=jnp.int32) * LANES
    lo = jnp.arange(LANES, dtype=jnp.int32)
    ang_hi = _dft_angle(k[:, None], hi[None, :], seq)
    ang_lo = _dft_angle(k[:, None], lo[None, :], seq)
    return chan, jnp.cos(ang_hi), jnp.sin(ang_hi), jnp.cos(ang_lo), jnp.sin(ang_lo)


def _fourier_chan_kernel(x_ref, mod_ref, g_ref, chan_ref, o_ref, *, bb, tt):
    x = x_ref[...].reshape(bb * tt, D)
    mod = mod_ref[0]
    h = _modulate(x, g_ref[...], mod[0:1], mod[1:2]).astype(BF16)
    for gi in range(D // FOURIER_GROUP_DIM):
        cols = slice(gi * LANES, (gi + 1) * LANES)
        cs = _dot(h[:, cols], chan_ref[...]).astype(BF16)
        for b in range(bb):
            rows = slice(b * tt, (b + 1) * tt)
            o_ref[b, 0, :, cols] = cs[rows, 0:LANES]
            o_ref[b, 1, :, cols] = cs[rows, LANES:2 * LANES]


def _fourier_chan(x, mod, g, chan, *, bb, tt):
    nb, seq, _ = x.shape
    n_groups = mod.shape[0]
    return pl.pallas_call(
        functools.partial(_fourier_chan_kernel, bb=bb, tt=tt),
        grid=(nb // bb, seq // tt),
        in_specs=[pl.BlockSpec((bb, tt, D), lambda b, t: (b, t, 0)),
                  pl.BlockSpec((1, 6, D), lambda b, t: ((b * bb * n_groups) // nb, 0, 0)),
                  _resident((1, D)), _resident(chan.shape)],
        out_specs=pl.BlockSpec((bb, 2, tt, D), lambda b, t: (b, 0, t, 0)),
        out_shape=jax.ShapeDtypeStruct((nb, 2, seq, D), BF16),
        compiler_params=_params(2),
        name="fourier_chan",
    )(x, mod, g, chan)


def _fourier_pos_kernel(x_ref, mod_ref, hh_ref, chi_ref, shi_ref, clo_ref, slo_ref, wout_ref,
                        o_ref, lhs_ref, *, seq, scale):
    clo = clo_ref[...]
    slo = slo_ref[...]
    for a in range(seq // LANES):
        ca = chi_ref[:, a:a + 1]
        sa = shi_ref[:, a:a + 1]
        lhs_ref[:, a * LANES:(a + 1) * LANES] = (ca * clo - sa * slo).astype(BF16)
        lhs_ref[:, seq + a * LANES:seq + (a + 1) * LANES] = (-(sa * clo + ca * slo)).astype(BF16)
    f = (_dot(lhs_ref[...], hh_ref[0]) * scale).astype(BF16)
    mod = mod_ref[0]
    o_ref[0] = x_ref[0] + mod[2:3] * _dot(f, wout_ref[...])


def _fourier_pos(x, mod, hh, chi, shi, clo, slo, wout, *, tr):
    nb, seq, _ = x.shape
    n_groups = mod.shape[0]
    n_hi = seq // LANES
    scale = 1.0 / math.sqrt(seq * FOURIER_GROUP_DIM)
    return pl.pallas_call(
        functools.partial(_fourier_pos_kernel, seq=seq, scale=scale),
        grid=(nb, seq // tr),
        in_specs=[pl.BlockSpec((1, tr, D), lambda b, t: (b, t, 0)),
                  pl.BlockSpec((1, 6, D), lambda b, t: ((b * n_groups) // nb, 0, 0)),
                  pl.BlockSpec((1, 2 * seq, D), lambda b, t: (b, 0, 0)),
                  pl.BlockSpec((tr, n_hi), lambda b, t: (t, 0)),
                  pl.BlockSpec((tr, n_hi), lambda b, t: (t, 0)),
                  pl.BlockSpec((tr, LANES), lambda b, t: (t, 0)),
                  pl.BlockSpec((tr, LANES), lambda b, t: (t, 0)),
                  _resident(wout.shape)],
        out_specs=pl.BlockSpec((1, tr, D), lambda b, t: (b, t, 0)),
        out_shape=jax.ShapeDtypeStruct((nb, seq, D), F32),
        scratch_shapes=[pltpu.VMEM((tr, 2 * seq), BF16)],
        compiler_params=_params(2),
        name="fourier_pos",
    )(x, mod, hh, chi, shi, clo, slo, wout)


def _fourier(x, mod, g, wout, *, bb, tt, tr):
    nb, seq, _ = x.shape
    chan, chi, shi, clo, slo = _fourier_tables(seq)
    hh = _fourier_chan(x, mod, g, chan, bb=bb, tt=tt)
    return _fourier_pos(x, mod, hh.reshape(nb, 2 * seq, D), chi, shi, clo, slo, wout, tr=tr)


def _fourier_short_kernel(x_ref, mod_ref, g_ref, chan_ref, chi_ref, shi_ref, clo_ref, slo_ref,
                          wout_ref, o_ref, lhs_ref, hh_ref, f_ref, *, bb, seq, scale):
    @pl.when(pl.program_id(0) == 0)
    def _():
        clo = clo_ref[...]
        slo = slo_ref[...]
        for a in range(seq // LANES):
            ca = chi_ref[:, a:a + 1]
            sa = shi_ref[:, a:a + 1]
            lhs_ref[:, a * LANES:(a + 1) * LANES] = (ca * clo - sa * slo).astype(BF16)
            lhs_ref[:, seq + a * LANES:seq + (a + 1) * LANES] = (-(sa * clo + ca * slo)).astype(BF16)

    x = x_ref[...]
    mod = mod_ref[0]
    h = _modulate(x, g_ref[...], mod[0:1], mod[1:2]).astype(BF16)
    for gi in range(D // FOURIER_GROUP_DIM):
        cols = slice(gi * LANES, (gi + 1) * LANES)
        cs = _dot(h[:, cols], chan_ref[...]).astype(BF16)
        for b in range(bb):
            rows = slice(b * seq, (b + 1) * seq)
            hh_ref[b, 0:seq, cols] = cs[rows, 0:LANES]
            hh_ref[b, seq:2 * seq, cols] = cs[rows, LANES:2 * LANES]
    for b in range(bb):
        f_ref[b * seq:(b + 1) * seq, :] = (_dot(lhs_ref[...], hh_ref[b]) * scale).astype(BF16)
    o_ref[...] = x + mod[2:3] * _dot(f_ref[...], wout_ref[...])


def _fourier_short(x, mod, g, wout, *, seq, bb):
    n = x.shape[0]
    tm = bb * seq
    chan, chi, shi, clo, slo = _fourier_tables(seq)
    scale = 1.0 / math.sqrt(seq * FOURIER_GROUP_DIM)
    return pl.pallas_call(
        functools.partial(_fourier_short_kernel, bb=bb, seq=seq, scale=scale),
        grid=(n // tm,),
        in_specs=[pl.BlockSpec((tm, D), lambda i: (i, 0)), _mod_spec(n, mod.shape[0], tm),
                  _resident((1, D)), _resident(chan.shape), _resident(chi.shape),
                  _resident(shi.shape), _resident(clo.shape), _resident(slo.shape),
                  _resident(wout.shape)],
        out_specs=pl.BlockSpec((tm, D), lambda i: (i, 0)),
        out_shape=jax.ShapeDtypeStruct((n, D), F32),
        scratch_shapes=[pltpu.VMEM((seq, 2 * seq), BF16), pltpu.VMEM((bb, 2 * seq, D), BF16),
                        pltpu.VMEM((tm, D), BF16)],
        compiler_params=_params(1),
        name="fourier_short",
    )(x, mod, g, chan, chi, shi, clo, slo, wout)


def kernel(x_prompt, x_sample, cache_k, cache_v, c, c_ctx, ada_w, ada_b, norm_mix_g, norm_ffn_g,
           final_g, attn_wqkv, attn_wo, attn_sink, sgu_w_in, sgu_ln_g, sgu_w_s, sgu_b_s,
           sgu_w_out, sc_w_in, sc_conv, sc_w_out, fn_w_out, ffn_w_up, ffn_conv, ffn_w_down):
    nbp, seq_p, _ = x_prompt.shape
    nbs, seq_s, _ = x_sample.shape
    depth = ada_w.shape[0]
    tm = 512

    cvec = jnp.concatenate([c_ctx[None, :], c, jnp.zeros((8 - 1 - nbs, D), F32)], axis=0)
    mods = _ada_mod(cvec, ada_w, ada_b).reshape(depth, 8, 6, D)

    xp = x_prompt.reshape(nbp * seq_p, D)
    xs = x_sample.reshape(nbs * seq_s, D)
    new_k = new_v = None
    wup = ffn_w_up.astype(BF16)
    wdn = ffn_w_down.astype(BF16)
    g_ffn = norm_ffn_g[:, None, :]

    for l in range(depth):
        kind, j = l % 4, l // 4
        mp = mods[l, 0:1]
        ms = mods[l, 1:1 + nbs]
        g_mix = norm_mix_g[l][None, :]
        if kind == 0:
            wqkv = attn_wqkv[j].astype(BF16)
            wo = attn_wo[j].astype(BF16)
            sink = attn_sink[j][None, :]
            xp, new_k, new_v = _attn_prompt(xp, mp, g_mix, wqkv, wo, sink, seq=seq_p, bb=2)
            cos, sin = _rope_tables(seq_s)
            q, k, v = _qkv_rope(xs, ms, g_mix, wqkv, cos, sin, seq=seq_s, tm=tm)
            past = cache_k.shape[2]
            ck = cache_k[:, j].reshape(nbs, past, KV_DIM).astype(BF16)
            cv = cache_v[:, j].reshape(nbs, past, KV_DIM).astype(BF16)
            xs = _attn_latent(xs.reshape(nbs, seq_s, D), ms, q.reshape(nbs, seq_s, N_HEADS * HD),
                              k.reshape(nbs, seq_s, KV_DIM), v.reshape(nbs, seq_s, KV_DIM),
                              ck, cv, wo, sink, tq=256).reshape(nbs * seq_s, D)
        elif kind == 1:
            win = sgu_w_in[j].astype(BF16)
            ws = sgu_w_s[j].astype(BF16)
            wout = sgu_w_out[j].astype(BF16)
            lng = sgu_ln_g[j][None, :]
            bs_full = jnp.repeat(sgu_b_s[j].T, LANES, axis=1)
            xp = _sgu(xp, mp, g_mix, win, lng, ws, bs_full, wout, tm=tm)
            xs = _sgu(xs, ms, g_mix, win, lng, ws, bs_full, wout, tm=tm)
        elif kind == 2:
            win = sc_w_in[j].astype(BF16)
            wout = sc_w_out[j].astype(BF16)
            xp = _sconv(xp, mp, g_mix, win, sc_conv[j], wout, seq=seq_p, tm=tm)
            xs = _sconv(xs, ms, g_mix, win, sc_conv[j], wout, seq=seq_s, tm=tm)
        else:
            wout = fn_w_out[j].astype(BF16)
            xp = _fourier_short(xp, mp, g_mix, wout, seq=seq_p, bb=4)
            xs = _fourier(xs.reshape(nbs, seq_s, D), ms, g_mix, wout,
                          bb=1, tt=tm, tr=tm).reshape(nbs * seq_s, D)

        fg = final_g[None, :] if l == depth - 1 else None
        xp = _ffn(xp, mp, g_ffn, wup, ffn_conv, wdn, fg, layer=l, seq=seq_p, tm=tm)
        xs = _ffn(xs, ms, g_ffn, wup, ffn_conv, wdn, fg, layer=l, seq=seq_s, tm=tm)

    y_prompt = xp.reshape(nbp, seq_p, D)
    y_sample = xs.reshape(nbs, seq_s, D)
    new_cache_k = new_k.reshape(nbp, 1, seq_p, N_KV, HD)
    new_cache_v = new_v.reshape(nbp, 1, seq_p, N_KV, HD)
    return (y_prompt, y_sample, new_cache_k, new_cache_v)
```

```python
import functools
import math

import jax
import jax.numpy as jnp
from jax import lax
from jax.experimental import pallas as pl
from jax.experimental.pallas import tpu as pltpu

D = 1024
N_HEADS = 16
N_KV = 4
HD = 64
Q_PER_KV = N_HEADS // N_KV
KV_DIM = N_KV * HD
WINDOW = 128
GRID_W = 64
ROPE_THETA = 10000.0
SGU_CHUNK = 128
SGU_GROUPS = 8
FOURIER_GROUP_DIM = 128
D_FF = 2816
EPS = 1e-6
NEG_BIG = -1e30

LANES = 128
HALO_ROWS = 8
FF_CHUNK = 256
UNIT = 256
DFT_BLOCK = 512
VMEM_LIMIT = 52 * 1024 * 1024

F32 = jnp.float32
BF16 = jnp.bfloat16


def _resident(shape):
    nd = len(shape)
    return pl.BlockSpec(shape, lambda *_: (0,) * nd, pipeline_mode=pl.Buffered(1))


def _params(n_axes, flags=None):
    return pltpu.CompilerParams(dimension_semantics=("arbitrary",) * n_axes,
                                vmem_limit_bytes=VMEM_LIMIT, flags=flags)


def _modulate(x, g, shift, scale):
    ms = jnp.mean(x * x, axis=-1, keepdims=True)
    return (x * lax.rsqrt(ms + EPS)) * (g * (1.0 + scale)) + shift


def _rms(x, g):
    ms = jnp.mean(x * x, axis=-1, keepdims=True)
    return (x * lax.rsqrt(ms + EPS)) * g


def _dot(a, b):
    return jnp.dot(a, b, preferred_element_type=F32)


def _dot_nt(a, b):
    return lax.dot_general(a, b, (((1,), (1,)), ((), ())), preferred_element_type=F32)


def _ada_kernel(c_ref, w_ref, b_ref, o_ref):
    c = c_ref[...]
    s = (c * jax.nn.sigmoid(c)).astype(BF16)
    o_ref[0] = _dot(s, w_ref[0].astype(BF16)) + b_ref[0]


def _ada_mod(cvec, ada_w, ada_b):
    depth, _, n6 = ada_w.shape
    bn = D
    return pl.pallas_call(
        _ada_kernel,
        grid=(depth, n6 // bn),
        in_specs=[pl.BlockSpec((8, D), lambda l, j: (0, 0)),
                  pl.BlockSpec((1, D, bn), lambda l, j: (l, 0, j)),
                  pl.BlockSpec((1, 1, bn), lambda l, j: (l, 0, j))],
        out_specs=pl.BlockSpec((1, 8, bn), lambda l, j: (l, 0, j)),
        out_shape=jax.ShapeDtypeStruct((depth, 8, n6), F32),
        compiler_params=_params(2),
        name="ada_mod",
    )(cvec, ada_w, ada_b.reshape(depth, 1, n6))


def _halo_specs(n_rows, tm):
    t8 = tm // HALO_ROWS
    last = n_rows // HALO_ROWS - 1
    return [pl.BlockSpec((HALO_ROWS, D), lambda i: (jnp.maximum(i * t8 - 1, 0), 0)),
            pl.BlockSpec((HALO_ROWS, D), lambda i: (jnp.minimum((i + 1) * t8, last), 0))]


def _mod_spec(n_rows, n_groups, tm):
    rows_per_group = n_rows // n_groups
    return pl.BlockSpec((1, 6, D), lambda i: ((i * tm) // rows_per_group, 0, 0))


class _Units:
    def __init__(self, n_rows, seq, tm):
        assert n_rows % tm == 0 and tm % UNIT == 0
        self.tm, self.seq = tm, seq
        self.n = tm // UNIT
        self.halo = seq > UNIT
        assert seq == UNIT or (seq > tm and seq % tm == 0)
        self.ext_h = UNIT + 2 * HALO_ROWS if self.halo else UNIT
        self.ext_a = UNIT + 2 * HALO_ROWS

    def fill_hext(self, hext_ref, x, xp_ref, xn_ref, g, shift, scale, tile_idx):
        h = _modulate(x, g, shift, scale)
        for u in range(self.n):
            hext_ref[u, 0:UNIT, :] = h[u * UNIT:(u + 1) * UNIT].astype(BF16)
        if not self.halo:
            return
        tiles_per_seq = self.seq // self.tm
        at_start = (tile_idx % tiles_per_seq) == 0
        at_end = (tile_idx % tiles_per_seq) == tiles_per_seq - 1
        for u in range(self.n):
            if u > 0:
                prev = h[u * UNIT - HALO_ROWS:u * UNIT]
            else:
                prev = jnp.where(at_start, 0.0, _modulate(xp_ref[...], g, shift, scale))
            if u < self.n - 1:
                nxt = h[(u + 1) * UNIT:(u + 1) * UNIT + HALO_ROWS]
            else:
                nxt = jnp.where(at_end, 0.0, _modulate(xn_ref[...], g, shift, scale))
            hext_ref[u, UNIT:UNIT + HALO_ROWS, :] = prev.astype(BF16)
            hext_ref[u, UNIT + HALO_ROWS:UNIT + 2 * HALO_ROWS, :] = nxt.astype(BF16)


    def zero_gaps(self, ref):
        zeros = jnp.zeros((HALO_ROWS, LANES), ref.dtype)
        for c in range(ref.shape[0]):
            ref[c, 0:HALO_ROWS, :] = zeros
            ref[c, HALO_ROWS + UNIT:2 * HALO_ROWS + UNIT, :] = zeros

    def store_padded(self, ref, r):
        for c in range(ref.shape[0]):
            cols = slice(c * LANES, (c + 1) * LANES)
            ref[c, HALO_ROWS:HALO_ROWS + UNIT, :] = r[0:UNIT, cols]
            if self.halo:
                ref[c, 0:HALO_ROWS, :] = r[UNIT:UNIT + HALO_ROWS, cols]
                ref[c, HALO_ROWS + UNIT:2 * HALO_ROWS + UNIT, :] = (
                    r[UNIT + HALO_ROWS:UNIT + 2 * HALO_ROWS, cols])

    def conv3(self, ref, w):
        b = HALO_ROWS
        out = []
        for c in range(ref.shape[0]):
            wc = w[:, c * LANES:(c + 1) * LANES]
            out.append(ref[c, b - 1:b - 1 + UNIT, :] * wc[0:1] + ref[c, b:b + UNIT, :] * wc[1:2]
                       + ref[c, b + 1:b + 1 + UNIT, :] * wc[2:3])
        return jnp.concatenate(out, axis=1)


def _ffn_kernel(*refs, units, final):
    halo, n_units = units.halo, units.n
    it = iter(refs)
    x_ref = next(it)
    xp_ref = next(it) if halo else None
    xn_ref = next(it) if halo else None
    mod_ref, g_ref, wup_ref, cw_ref, wdn_ref = (next(it) for _ in range(5))
    fg_ref = next(it) if final else None
    o_ref, hext_ref, acc_ref = (next(it) for _ in range(3))
    a_refs = [[[next(it), next(it)] for _ in range(n_units)] for _ in range(2)]

    x = x_ref[...]
    mod = mod_ref[0]
    units.fill_hext(hext_ref, x, xp_ref, xn_ref, g_ref[...], mod[3:4], mod[4:5], pl.program_id(0))
    if not halo:
        for p in range(2):
            for u in range(n_units):
                for t in range(2):
                    units.zero_gaps(a_refs[p][u][t])

    def up(u, j):
        c0 = j * FF_CHUNK
        hx = hext_ref[u]
        for t, col in enumerate((c0, D_FF + c0)):
            units.store_padded(a_refs[j % 2][u][t], _dot(hx, wup_ref[:, col:col + FF_CHUNK]))

    n_chunks = D_FF // FF_CHUNK
    for u in range(n_units):
        up(u, 0)
    for j in range(n_chunks):
        c0 = j * FF_CHUNK
        p = j % 2
        for u in range(n_units):
            if j + 1 < n_chunks:
                up(u, j + 1)
            cg = units.conv3(a_refs[p][u][0], cw_ref[:, c0:c0 + FF_CHUNK])
            cu = units.conv3(a_refs[p][u][1], cw_ref[:, D_FF + c0:D_FF + c0 + FF_CHUNK])
            act = ((cg * jax.nn.sigmoid(cg)) * cu).astype(BF16)
            part = _dot(act, wdn_ref[c0:c0 + FF_CHUNK, :])
            rows = slice(u * UNIT, (u + 1) * UNIT)
            if j == 0:
                acc_ref[rows, :] = part
            else:
                acc_ref[rows, :] += part

    y = x + mod[5:6] * acc_ref[...]
    if final:
        y = _rms(y, fg_ref[...])
    o_ref[...] = y


def _layer_resident(arr, layer):
    nd = arr.ndim - 1
    return pl.BlockSpec((None,) + arr.shape[1:], lambda *_: (layer,) + (0,) * nd,
                        pipeline_mode=pl.Buffered(1))


def _ffn(x, mod, g, wup, cw, wdn, final_g, *, layer, seq, tm):
    n = x.shape[0]
    units = _Units(n, seq, tm)
    halo = units.halo
    final = final_g is not None
    in_specs = [pl.BlockSpec((tm, D), lambda i: (i, 0))]
    args = [x]
    if halo:
        in_specs += _halo_specs(n, tm)
        args += [x, x]
    in_specs += [_mod_spec(n, mod.shape[0], tm), _layer_resident(g, layer),
                 _layer_resident(wup, layer), _layer_resident(cw, layer),
                 _layer_resident(wdn, layer)]
    args += [mod, g, wup, cw, wdn]
    if final:
        in_specs.append(_resident((1, D)))
        args.append(final_g)
    scratch = [pltpu.VMEM((units.n, units.ext_h, D), BF16), pltpu.VMEM((tm, D), F32)]
    scratch += [pltpu.VMEM((FF_CHUNK // LANES, units.ext_a, LANES), F32)] * (4 * units.n)
    return pl.pallas_call(
        functools.partial(_ffn_kernel, units=units, final=final),
        grid=(n // tm,),
        in_specs=in_specs,
        out_specs=pl.BlockSpec((tm, D), lambda i: (i, 0)),
        out_shape=jax.ShapeDtypeStruct((n, D), F32),
        scratch_shapes=scratch,
        compiler_params=_params(1),
        name="conv_ffn",
    )(*args)


def _sconv_kernel(*refs, units):
    halo, n_units = units.halo, units.n
    it = iter(refs)
    x_ref = next(it)
    xp_ref = next(it) if halo else None
    xn_ref = next(it) if halo else None
    mod_ref, g_ref, win_ref, cw_ref, wout_ref, o_ref, hext_ref, t_ref = (next(it) for _ in range(8))
    p_refs = [[next(it) for _ in range(n_units)] for _ in range(2)]

    x = x_ref[...]
    mod = mod_ref[0]
    units.fill_hext(hext_ref, x, xp_ref, xn_ref, g_ref[...], mod[0:1], mod[1:2], pl.program_id(0))
    if not halo:
        for p in range(2):
            for u in range(n_units):
                units.zero_gaps(p_refs[p][u])

    for j in range(D // FF_CHUNK):
        c0 = j * FF_CHUNK
        for u in range(n_units):
            hx = hext_ref[u]
            b = _dot(hx, win_ref[:, c0:c0 + FF_CHUNK])
            cg = _dot(hx, win_ref[:, D + c0:D + c0 + FF_CHUNK])
            xin = _dot(hx, win_ref[:, 2 * D + c0:2 * D + c0 + FF_CHUNK])
            units.store_padded(p_refs[j % 2][u], cg * xin)
            y = units.conv3(p_refs[j % 2][u], cw_ref[:, c0:c0 + FF_CHUNK])
            t_ref[u * UNIT:(u + 1) * UNIT, c0:c0 + FF_CHUNK] = (b[0:UNIT] * y).astype(BF16)

    o_ref[...] = x + mod[2:3] * _dot(t_ref[...], wout_ref[...])


def _sconv(x, mod, g, win, cw, wout, *, seq, tm):
    n = x.shape[0]
    units = _Units(n, seq, tm)
    halo = units.halo
    in_specs = [pl.BlockSpec((tm, D), lambda i: (i, 0))]
    args = [x]
    if halo:
        in_specs += _halo_specs(n, tm)
        args += [x, x]
    in_specs += [_mod_spec(n, mod.shape[0], tm), _resident((1, D)), _resident(win.shape),
                 _resident(cw.shape), _resident(wout.shape)]
    args += [mod, g, win, cw, wout]
    scratch = [pltpu.VMEM((units.n, units.ext_h, D), BF16), pltpu.VMEM((tm, D), BF16)]
    scratch += [pltpu.VMEM((FF_CHUNK // LANES, units.ext_a, LANES), F32)] * (2 * units.n)
    return pl.pallas_call(
        functools.partial(_sconv_kernel, units=units),
        grid=(n // tm,),
        in_specs=in_specs,
        out_specs=pl.BlockSpec((tm, D), lambda i: (i, 0)),
        out_shape=jax.ShapeDtypeStruct((n, D), F32),
        scratch_shapes=scratch,
        compiler_params=_params(1),
        name="short_conv",
    )(*args)


def _gelu_tanh(x):
    return 0.5 * x * (1.0 + jnp.tanh(math.sqrt(2.0 / math.pi) * (x + 0.044715 * (x * x * x))))


def _sgu_kernel(x_ref, mod_ref, g_ref, win_ref, lng_ref, ws_ref, bs_ref, wout_ref, o_ref,
                v_ref, t_ref, *, tm):
    x = x_ref[...]
    mod = mod_ref[0]
    h = _modulate(x, g_ref[...], mod[0:1], mod[1:2]).astype(BF16)
    v = _gelu_tanh(_dot(h, win_ref[:, D:2 * D]))
    mu = jnp.mean(v, axis=-1, keepdims=True)
    vc = v - mu
    var = jnp.mean(vc * vc, axis=-1, keepdims=True)
    v_ref[...] = ((vc * lax.rsqrt(var + EPS)) * lng_ref[...]).astype(BF16)

    n_chunks = tm // SGU_CHUNK
    for gi in range(SGU_GROUPS):
        cols = slice(gi * LANES, (gi + 1) * LANES)
        if gi % 2 == 0:
            u_pair = _gelu_tanh(_dot(h, win_ref[:, gi * LANES:(gi + 2) * LANES]))
        u = u_pair[:, (gi % 2) * LANES:(gi % 2 + 1) * LANES]
        rhs = jnp.concatenate(
            [v_ref[n * SGU_CHUNK:(n + 1) * SGU_CHUNK, cols] for n in range(n_chunks)], axis=1)
        mixed = _dot(ws_ref[gi], rhs)
        bias = bs_ref[:, cols]
        for n in range(n_chunks):
            rows = slice(n * SGU_CHUNK, (n + 1) * SGU_CHUNK)
            t_ref[rows, cols] = (u[rows] * (mixed[:, n * LANES:(n + 1) * LANES] + bias)).astype(BF16)

    o_ref[...] = x + mod[2:3] * _dot(t_ref[...], wout_ref[...])


def _sgu(x, mod, g, win, lng, ws, bs_full, wout, *, tm):
    n = x.shape[0]
    return pl.pallas_call(
        functools.partial(_sgu_kernel, tm=tm),
        grid=(n // tm,),
        in_specs=[pl.BlockSpec((tm, D), lambda i: (i, 0)), _mod_spec(n, mod.shape[0], tm),
                  _resident((1, D)), _resident(win.shape), _resident((1, D)), _resident(ws.shape),
                  _resident(bs_full.shape), _resident(wout.shape)],
        out_specs=pl.BlockSpec((tm, D), lambda i: (i, 0)),
        out_shape=jax.ShapeDtypeStruct((n, D), F32),
        scratch_shapes=[pltpu.VMEM((tm, D), BF16), pltpu.VMEM((tm, D), BF16)],
        compiler_params=_params(1),
        name="sgu",
    )(x, mod, g, win, lng, ws, bs_full, wout)


LOG2E = math.log2(math.e)
Q_SCALE = HD ** -0.5 * LOG2E


def _softmax_pv_t(parts, sink, v_t):
    sink2 = sink * LOG2E
    m = sink2
    for s_t in parts:
        m = jnp.maximum(m, jnp.max(s_t, axis=0, keepdims=True))
    den = jnp.exp2(sink2 - m)
    ps = []
    for s_t in parts:
        p = jnp.exp2(s_t - m)
        den = den + jnp.sum(p, axis=0, keepdims=True)
        ps.append(p.astype(BF16))
    p_all = ps[0] if len(ps) == 1 else jnp.concatenate(ps, axis=0)
    return _dot(v_t, p_all) / den


def _attn_prompt_kernel(x_ref, mod_ref, g_ref, wqkv_ref, wo_ref, sink_ref, o_ref, k_ref, v_ref,
                        q_s, k_s, ot_s, o_s, *, bb, seq):
    x = x_ref[...]
    mod = mod_ref[0]
    h = _modulate(x, g_ref[...], mod[0:1], mod[1:2]).astype(BF16)
    nq = N_HEADS * HD
    q_s[...] = (_dot(h, wqkv_ref[:, 0:nq]) * Q_SCALE).astype(BF16)
    k = _dot(h, wqkv_ref[:, nq:nq + KV_DIM])
    v = _dot(h, wqkv_ref[:, nq + KV_DIM:nq + 2 * KV_DIM])
    k_ref[...] = k
    v_ref[...] = v
    k_s[...] = k.astype(BF16)

    def one_batch(b, carry):
        rows = pl.ds(pl.multiple_of(b * seq, seq), seq)
        v_t = v_ref[rows, :].T.astype(BF16)
        for kv in range(N_KV):
            kh = k_s[rows, kv * HD:(kv + 1) * HD]
            vh_t = v_t[kv * HD:(kv + 1) * HD, :]
            scores = [_dot_nt(kh, q_s[rows, hd * HD:(hd + 1) * HD])
                      for hd in range(kv * Q_PER_KV, (kv + 1) * Q_PER_KV)]
            for j, s_t in enumerate(scores):
                hd = kv * Q_PER_KV + j
                ot_s[hd * HD:(hd + 1) * HD, :] = _softmax_pv_t([s_t], sink_ref[0, hd], vh_t)
        o_s[rows, :] = ot_s[...].T.astype(BF16)
        return carry

    lax.fori_loop(0, bb, one_batch, 0)
    o_ref[...] = x + mod[2:3] * _dot(o_s[...], wo_ref[...])


def _attn_prompt(x, mod, g, wqkv, wo, sink, *, seq, bb):
    n = x.shape[0]
    tm = bb * seq
    nq = N_HEADS * HD
    return pl.pallas_call(
        functools.partial(_attn_prompt_kernel, bb=bb, seq=seq),
        grid=(n // tm,),
        in_specs=[pl.BlockSpec((tm, D), lambda i: (i, 0)), _mod_spec(n, mod.shape[0], tm),
                  _resident((1, D)), _resident(wqkv.shape), _resident(wo.shape),
                  pl.BlockSpec(memory_space=pltpu.SMEM)],
        out_specs=[pl.BlockSpec((tm, D), lambda i: (i, 0)),
                   pl.BlockSpec((tm, KV_DIM), lambda i: (i, 0)),
                   pl.BlockSpec((tm, KV_DIM), lambda i: (i, 0))],
        out_shape=[jax.ShapeDtypeStruct((n, D), F32),
                   jax.ShapeDtypeStruct((n, KV_DIM), F32),
                   jax.ShapeDtypeStruct((n, KV_DIM), F32)],
        scratch_shapes=[pltpu.VMEM((tm, nq), BF16), pltpu.VMEM((tm, KV_DIM), BF16),
                        pltpu.VMEM((nq, seq), F32), pltpu.VMEM((tm, nq), BF16)],
        compiler_params=_params(1),
        name="attn_prompt",
    )(x, mod, g, wqkv, wo, sink)


def _rope_tables(seq):
    t = jnp.arange(seq)
    half = HD // 4
    inv = ROPE_THETA ** (-jnp.arange(half, dtype=F32) / half)
    ang_row = (t // GRID_W).astype(F32)[:, None] * inv[None, :]
    ang_col = (t % GRID_W).astype(F32)[:, None] * inv[None, :]
    cos_h = jnp.concatenate([jnp.cos(ang_row)] * 2 + [jnp.cos(ang_col)] * 2, axis=-1)
    sin_h = jnp.concatenate([-jnp.sin(ang_row), jnp.sin(ang_row),
                             -jnp.sin(ang_col), jnp.sin(ang_col)], axis=-1)
    reps = LANES // HD
    return jnp.tile(cos_h, (1, reps)), jnp.tile(sin_h, (1, reps))


def _qkv_rope_kernel(x_ref, mod_ref, g_ref, wqkv_ref, cos_ref, sin_ref, q_ref, k_ref, v_ref):
    x = x_ref[...]
    mod = mod_ref[0]
    h = _modulate(x, g_ref[...], mod[0:1], mod[1:2]).astype(BF16)
    nq = N_HEADS * HD
    cos = cos_ref[...]
    sin = sin_ref[...]
    lane = lax.broadcasted_iota(jnp.int32, cos.shape, 1)
    low = (lane % (HD // 2)) < (HD // 4)
    for blk in range((nq + KV_DIM) // LANES):
        c0 = blk * LANES
        if blk % 2 == 0:
            pair = _dot(h, wqkv_ref[:, c0:c0 + 2 * LANES])
        a = pair[:, (blk % 2) * LANES:(blk % 2 + 1) * LANES]
        partner = jnp.where(low, pltpu.roll(a, LANES - HD // 4, 1), pltpu.roll(a, HD // 4, 1))
        r = a * cos + partner * sin
        if c0 < nq:
            q_ref[:, c0:c0 + LANES] = (r * Q_SCALE).astype(BF16)
        else:
            k_ref[:, c0 - nq:c0 - nq + LANES] = r.astype(BF16)
    v_ref[...] = _dot(h, wqkv_ref[:, nq + KV_DIM:nq + 2 * KV_DIM]).astype(BF16)


def _qkv_rope(x, mod, g, wqkv, cos, sin, *, seq, tm):
    n = x.shape[0]
    nq = N_HEADS * HD
    tiles_per_seq = seq // tm
    return pl.pallas_call(
        _qkv_rope_kernel,
        grid=(n // tm,),
        in_specs=[pl.BlockSpec((tm, D), lambda i: (i, 0)), _mod_spec(n, mod.shape[0], tm),
                  _resident((1, D)), _resident(wqkv.shape),
                  pl.BlockSpec((tm, LANES), lambda i: (i % tiles_per_seq, 0)),
                  pl.BlockSpec((tm, LANES), lambda i: (i % tiles_per_seq, 0))],
        out_specs=[pl.BlockSpec((tm, nq), lambda i: (i, 0)),
                   pl.BlockSpec((tm, KV_DIM), lambda i: (i, 0)),
                   pl.BlockSpec((tm, KV_DIM), lambda i: (i, 0))],
        out_shape=[jax.ShapeDtypeStruct((n, nq), BF16),
                   jax.ShapeDtypeStruct((n, KV_DIM), BF16),
                   jax.ShapeDtypeStruct((n, KV_DIM), BF16)],
        compiler_params=_params(1),
        name="qkv_rope",
    )(x, mod, g, wqkv, cos, sin)


def _attn_latent_kernel(x_ref, mod_ref, q_ref, k_ref, v_ref, ck_ref, cv_ref, wo_ref, sink_ref,
                        o_ref, ot_s, *, tq, seq, span):
    qt = pl.program_id(1)
    ws = pl.multiple_of(jnp.clip(qt * tq - WINDOW, 0, seq - span), WINDOW)
    past = ck_ref.shape[1]
    kpos = ws + lax.broadcasted_iota(jnp.int32, (span, tq), 0)
    qpos = qt * tq + lax.broadcasted_iota(jnp.int32, (span, tq), 1)
    valid = jnp.abs(qpos - kpos) <= WINDOW
    v_t = jnp.concatenate([v_ref[0, pl.ds(ws, span), :].astype(F32).T,
                           cv_ref[0].astype(F32).T], axis=1).astype(BF16)

    for kv in range(N_KV):
        lanes = slice(kv * HD, (kv + 1) * HD)
        kcat = jnp.concatenate([k_ref[0, pl.ds(ws, span), lanes], ck_ref[0, :, lanes]], axis=0)
        scores = [_dot_nt(kcat, q_ref[0, :, hd * HD:(hd + 1) * HD])
                  for hd in range(kv * Q_PER_KV, (kv + 1) * Q_PER_KV)]
        for j, s_t in enumerate(scores):
            hd = kv * Q_PER_KV + j
            parts = [jnp.where(valid, s_t[0:span], NEG_BIG), s_t[span:span + past]]
            ot_s[hd * HD:(hd + 1) * HD, :] = _softmax_pv_t(parts, sink_ref[0, hd], v_t[lanes, :])

    mod = mod_ref[0]
    o_ref[0] = x_ref[0] + mod[2:3] * _dot(ot_s[...].T.astype(BF16), wo_ref[...])


def _attn_latent(x, mod, q, k, v, ck, cv, wo, sink, *, tq):
    nb, seq, _ = x.shape
    nq = N_HEADS * HD
    past = ck.shape[1]
    span = tq + 2 * WINDOW
    return pl.pallas_call(
        functools.partial(_attn_latent_kernel, tq=tq, seq=seq, span=span),
        grid=(nb, seq // tq),
        in_specs=[pl.BlockSpec((1, tq, D), lambda b, t: (b, t, 0)),
                  pl.BlockSpec((1, 6, D), lambda b, t: (b, 0, 0)),
                  pl.BlockSpec((1, tq, nq), lambda b, t: (b, t, 0)),
                  pl.BlockSpec((1, seq, KV_DIM), lambda b, t: (b, 0, 0)),
                  pl.BlockSpec((1, seq, KV_DIM), lambda b, t: (b, 0, 0)),
                  pl.BlockSpec((1, past, KV_DIM), lambda b, t: (b, 0, 0)),
                  pl.BlockSpec((1, past, KV_DIM), lambda b, t: (b, 0, 0)),
                  _resident(wo.shape),
                  pl.BlockSpec(memory_space=pltpu.SMEM)],
        out_specs=pl.BlockSpec((1, tq, D), lambda b, t: (b, t, 0)),
        out_shape=jax.ShapeDtypeStruct((nb, seq, D), F32),
        scratch_shapes=[pltpu.VMEM((nq, tq), F32)],
        compiler_params=_params(2),
        name="attn_latent",
    )(x, mod, q, k, v, ck, cv, wo, sink)


def _dft_angle(k, t, period):
    return ((k * t) % period).astype(F32) * (2.0 * math.pi / period)


def _fourier_tables(seq):
    c = jnp.arange(FOURIER_GROUP_DIM, dtype=jnp.int32)
    ang_c = _dft_angle(c[:, None], c[None, :], FOURIER_GROUP_DIM)
    chan = jnp.concatenate([jnp.cos(ang_c), jnp.sin(ang_c)], axis=1).astype(BF16)
    k = jnp.arange(seq, dtype=jnp.int32)
    hi = jnp.arange(seq // LANES, dtype=jnp.int32) * LANES
    lo = jnp.arange(LANES, dtype=jnp.int32)
    ang_hi = _dft_angle(k[:, None], hi[None, :], seq)
    ang_lo = _dft_angle(k[:, None], lo[None, :], seq)
    return chan, jnp.cos(ang_hi), jnp.sin(ang_hi), jnp.cos(ang_lo), jnp.sin(ang_lo)


def _fourier_chan_kernel(x_ref, mod_ref, g_ref, chan_ref, o_ref, *, bb, tt):
    x = x_ref[...].reshape(bb * tt, D)
    mod = mod_ref[0]
    h = _modulate(x, g_ref[...], mod[0:1], mod[1:2]).astype(BF16)
    for gi in range(D // FOURIER_GROUP_DIM):
        cols = slice(gi * LANES, (gi + 1) * LANES)
        cs = _dot(h[:, cols], chan_ref[...]).astype(BF16)
        for b in range(bb):
            rows = slice(b * tt, (b + 1) * tt)
            o_ref[b, 0, :, cols] = cs[rows, 0:LANES]
            o_ref[b, 1, :, cols] = cs[rows, LANES:2 * LANES]


def _fourier_chan(x, mod, g, chan, *, bb, tt):
    nb, seq, _ = x.shape
    n_groups = mod.shape[0]
    return pl.pallas_call(
        functools.partial(_fourier_chan_kernel, bb=bb, tt=tt),
        grid=(nb // bb, seq // tt),
        in_specs=[pl.BlockSpec((bb, tt, D), lambda b, t: (b, t, 0)),
                  pl.BlockSpec((1, 6, D), lambda b, t: ((b * bb * n_groups) // nb, 0, 0)),
                  _resident((1, D)), _resident(chan.shape)],
        out_specs=pl.BlockSpec((bb, 2, tt, D), lambda b, t: (b, 0, t, 0)),
        out_shape=jax.ShapeDtypeStruct((nb, 2, seq, D), BF16),
        compiler_params=_params(2),
        name="fourier_chan",
    )(x, mod, g, chan)


def _fourier_pos_kernel(x_ref, mod_ref, hh_ref, chi_ref, shi_ref, clo_ref, slo_ref, wout_ref,
                        o_ref, lhs_ref, *, seq, scale):
    clo = clo_ref[...]
    slo = slo_ref[...]
    n_hi = seq // LANES
    blk = min(n_hi, DFT_BLOCK // LANES)
    f = None
    for a0 in range(0, n_hi, blk):
        for a in range(a0, a0 + blk):
            ca = chi_ref[:, a:a + 1]
            sa = shi_ref[:, a:a + 1]
            lhs_ref[:, a * LANES:(a + 1) * LANES] = (ca * clo - sa * slo).astype(BF16)
            lhs_ref[:, seq + a * LANES:seq + (a + 1) * LANES] = (
                -(sa * clo + ca * slo)).astype(BF16)
        for base in (0, seq):
            cols = slice(base + a0 * LANES, base + (a0 + blk) * LANES)
            part = _dot(lhs_ref[:, cols], hh_ref[0, cols, :])
            f = part if f is None else f + part
    f = (f * scale).astype(BF16)
    mod = mod_ref[0]
    o_ref[0] = x_ref[0] + mod[2:3] * _dot(f, wout_ref[...])


def _fourier_pos(x, mod, hh, chi, shi, clo, slo, wout, *, tr):
    nb, seq, _ = x.shape
    n_groups = mod.shape[0]
    n_hi = seq // LANES
    scale = 1.0 / math.sqrt(seq * FOURIER_GROUP_DIM)
    return pl.pallas_call(
        functools.partial(_fourier_pos_kernel, seq=seq, scale=scale),
        grid=(nb, seq // tr),
        in_specs=[pl.BlockSpec((1, tr, D), lambda b, t: (b, t, 0)),
                  pl.BlockSpec((1, 6, D), lambda b, t: ((b * n_groups) // nb, 0, 0)),
                  pl.BlockSpec((1, 2 * seq, D), lambda b, t: (b, 0, 0)),
                  pl.BlockSpec((tr, n_hi), lambda b, t: (t, 0)),
                  pl.BlockSpec((tr, n_hi), lambda b, t: (t, 0)),
                  pl.BlockSpec((tr, LANES), lambda b, t: (t, 0)),
                  pl.BlockSpec((tr, LANES), lambda b, t: (t, 0)),
                  _resident(wout.shape)],
        out_specs=pl.BlockSpec((1, tr, D), lambda b, t: (b, t, 0)),
        out_shape=jax.ShapeDtypeStruct((nb, seq, D), F32),
        scratch_shapes=[pltpu.VMEM((tr, 2 * seq), BF16)],
        compiler_params=_params(2),
        name="fourier_pos",
    )(x, mod, hh, chi, shi, clo, slo, wout)


def _fourier(x, mod, g, wout, *, bb, tt, tr):
    nb, seq, _ = x.shape
    chan, chi, shi, clo, slo = _fourier_tables(seq)
    hh = _fourier_chan(x, mod, g, chan, bb=bb, tt=tt)
    return _fourier_pos(x, mod, hh.reshape(nb, 2 * seq, D), chi, shi, clo, slo, wout, tr=tr)


def _fourier_short_kernel(x_ref, mod_ref, g_ref, chan_ref, chi_ref, shi_ref, clo_ref, slo_ref,
                          wout_ref, o_ref, lhs_ref, hh_ref, f_ref, *, bb, seq, scale):
    @pl.when(pl.program_id(0) == 0)
    def _():
        clo = clo_ref[...]
        slo = slo_ref[...]
        for a in range(seq // LANES):
            ca = chi_ref[:, a:a + 1]
            sa = shi_ref[:, a:a + 1]
            lhs_ref[:, a * LANES:(a + 1) * LANES] = (ca * clo - sa * slo).astype(BF16)
            lhs_ref[:, seq + a * LANES:seq + (a + 1) * LANES] = (-(sa * clo + ca * slo)).astype(BF16)

    x = x_ref[...]
    mod = mod_ref[0]
    h = _modulate(x, g_ref[...], mod[0:1], mod[1:2]).astype(BF16)
    for gi in range(D // FOURIER_GROUP_DIM):
        cols = slice(gi * LANES, (gi + 1) * LANES)
        cs = _dot(h[:, cols], chan_ref[...]).astype(BF16)
        for b in range(bb):
            rows = slice(b * seq, (b + 1) * seq)
            hh_ref[b, 0:seq, cols] = cs[rows, 0:LANES]
            hh_ref[b, seq:2 * seq, cols] = cs[rows, LANES:2 * LANES]
    for b in range(bb):
        f_ref[b * seq:(b + 1) * seq, :] = (_dot(lhs_ref[...], hh_ref[b]) * scale).astype(BF16)
    o_ref[...] = x + mod[2:3] * _dot(f_ref[...], wout_ref[...])


def _fourier_short(x, mod, g, wout, *, seq, bb):
    n = x.shape[0]
    tm = bb * seq
    chan, chi, shi, clo, slo = _fourier_tables(seq)
    scale = 1.0 / math.sqrt(seq * FOURIER_GROUP_DIM)
    return pl.pallas_call(
        functools.partial(_fourier_short_kernel, bb=bb, seq=seq, scale=scale),
        grid=(n // tm,),
        in_specs=[pl.BlockSpec((tm, D), lambda i: (i, 0)), _mod_spec(n, mod.shape[0], tm),
                  _resident((1, D)), _resident(chan.shape), _resident(chi.shape),
                  _resident(shi.shape), _resident(clo.shape), _resident(slo.shape),
                  _resident(wout.shape)],
        out_specs=pl.BlockSpec((tm, D), lambda i: (i, 0)),
        out_shape=jax.ShapeDtypeStruct((n, D), F32),
        scratch_shapes=[pltpu.VMEM((seq, 2 * seq), BF16), pltpu.VMEM((bb, 2 * seq, D), BF16),
                        pltpu.VMEM((tm, D), BF16)],
        compiler_params=_params(1),
        name="fourier_short",
    )(x, mod, g, chan, chi, shi, clo, slo, wout)


def kernel(x_prompt, x_sample, cache_k, cache_v, c, c_ctx, ada_w, ada_b, norm_mix_g, norm_ffn_g,
           final_g, attn_wqkv, attn_wo, attn_sink, sgu_w_in, sgu_ln_g, sgu_w_s, sgu_b_s,
           sgu_w_out, sc_w_in, sc_conv, sc_w_out, fn_w_out, ffn_w_up, ffn_conv, ffn_w_down):
    nbp, seq_p, _ = x_prompt.shape
    nbs, seq_s, _ = x_sample.shape
    depth = ada_w.shape[0]
    tm = 512

    cvec = jnp.concatenate([c_ctx[None, :], c, jnp.zeros((8 - 1 - nbs, D), F32)], axis=0)
    mods = _ada_mod(cvec, ada_w, ada_b).reshape(depth, 8, 6, D)

    xp = x_prompt.reshape(nbp * seq_p, D)
    xs = x_sample.reshape(nbs * seq_s, D)
    new_k = new_v = None
    wup = ffn_w_up.astype(BF16)
    wdn = ffn_w_down.astype(BF16)
    g_ffn = norm_ffn_g[:, None, :]

    for l in range(depth):
        kind, j = l % 4, l // 4
        mp = mods[l, 0:1]
        ms = mods[l, 1:1 + nbs]
        g_mix = norm_mix_g[l][None, :]
        if kind == 0:
            wqkv = attn_wqkv[j].astype(BF16)
            wo = attn_wo[j].astype(BF16)
            sink = attn_sink[j][None, :]
            xp, new_k, new_v = _attn_prompt(xp, mp, g_mix, wqkv, wo, sink, seq=seq_p, bb=4)
            cos, sin = _rope_tables(seq_s)
            q, k, v = _qkv_rope(xs, ms, g_mix, wqkv, cos, sin, seq=seq_s, tm=tm)
            past = cache_k.shape[2]
            ck = cache_k[:, j].reshape(nbs, past, KV_DIM).astype(BF16)
            cv = cache_v[:, j].reshape(nbs, past, KV_DIM).astype(BF16)
            xs = _attn_latent(xs.reshape(nbs, seq_s, D), ms, q.reshape(nbs, seq_s, N_HEADS * HD),
                              k.reshape(nbs, seq_s, KV_DIM), v.reshape(nbs, seq_s, KV_DIM),
                              ck, cv, wo, sink, tq=256).reshape(nbs * seq_s, D)
        elif kind == 1:
            win = sgu_w_in[j].astype(BF16)
            ws = sgu_w_s[j].astype(BF16)
            wout = sgu_w_out[j].astype(BF16)
            lng = sgu_ln_g[j][None, :]
            bs_full = jnp.repeat(sgu_b_s[j].T, LANES, axis=1)
            xp = _sgu(xp, mp, g_mix, win, lng, ws, bs_full, wout, tm=tm)
            xs = _sgu(xs, ms, g_mix, win, lng, ws, bs_full, wout, tm=tm)
        elif kind == 2:
            win = sc_w_in[j].astype(BF16)
            wout = sc_w_out[j].astype(BF16)
            xp = _sconv(xp, mp, g_mix, win, sc_conv[j], wout, seq=seq_p, tm=tm)
            xs = _sconv(xs, ms, g_mix, win, sc_conv[j], wout, seq=seq_s, tm=tm)
        else:
            wout = fn_w_out[j].astype(BF16)
            xp = _fourier_short(xp, mp, g_mix, wout, seq=seq_p, bb=4)
            xs = _fourier(xs.reshape(nbs, seq_s, D), ms, g_mix, wout,
                          bb=1, tt=tm, tr=tm).reshape(nbs * seq_s, D)

        fg = final_g[None, :] if l == depth - 1 else None
        xp = _ffn(xp, mp, g_ffn, wup, ffn_conv, wdn, fg, layer=l, seq=seq_p, tm=tm)
        xs = _ffn(xs, ms, g_ffn, wup, ffn_conv, wdn, fg, layer=l, seq=seq_s, tm=tm)

    y_prompt = xp.reshape(nbp, seq_p, D)
    y_sample = xs.reshape(nbs, seq_s, D)
    new_cache_k = new_k.reshape(nbp, 1, seq_p, N_KV, HD)
    new_cache_v = new_v.reshape(nbp, 1, seq_p, N_KV, HD)
    return (y_prompt, y_sample, new_cache_k, new_cache_v)
```

```python
import functools
import math
from typing import Callable, NamedTuple

import jax
import jax.numpy as jnp
from jax import lax
from jax.experimental import pallas as pl
from jax.experimental.pallas import tpu as pltpu

D = 1024
N_HEADS = 16
N_KV = 4
HD = 64
Q_PER_KV = N_HEADS // N_KV
KV_DIM = N_KV * HD
WINDOW = 128
GRID_W = 64
ROPE_THETA = 10000.0
SGU_CHUNK = 128
SGU_GROUPS = 8
FOURIER_GROUP_DIM = 128
D_FF = 2816
EPS = 1e-6
NEG_BIG = -1e30

LANES = 128
HALO_ROWS = 8
FF_CHUNK = 256
UNIT = 256
DFT_BLOCK = 512
VMEM_LIMIT = 52 * 1024 * 1024

F32 = jnp.float32
BF16 = jnp.bfloat16


def _resident(shape):
    nd = len(shape)
    return pl.BlockSpec(shape, lambda *_: (0,) * nd, pipeline_mode=pl.Buffered(1))


def _params(n_axes, flags=None):
    return pltpu.CompilerParams(dimension_semantics=("arbitrary",) * n_axes,
                                vmem_limit_bytes=VMEM_LIMIT, flags=flags)


def _modulate(x, g, shift, scale):
    ms = jnp.mean(x * x, axis=-1, keepdims=True)
    return (x * lax.rsqrt(ms + EPS)) * (g * (1.0 + scale)) + shift


def _rms(x, g):
    ms = jnp.mean(x * x, axis=-1, keepdims=True)
    return (x * lax.rsqrt(ms + EPS)) * g


def _dot(a, b):
    return jnp.dot(a, b, preferred_element_type=F32)


def _dot_nt(a, b):
    return lax.dot_general(a, b, (((1,), (1,)), ((), ())), preferred_element_type=F32)


def _ada_kernel(c_ref, w_ref, b_ref, o_ref):
    c = c_ref[...]
    s = (c * jax.nn.sigmoid(c)).astype(BF16)
    o_ref[0] = _dot(s, w_ref[0].astype(BF16)) + b_ref[0]


def _ada_mod(cvec, ada_w, ada_b):
    depth, _, n6 = ada_w.shape
    bn = 2 * D
    return pl.pallas_call(
        _ada_kernel,
        grid=(depth, n6 // bn),
        in_specs=[pl.BlockSpec((8, D), lambda l, j: (0, 0)),
                  pl.BlockSpec((1, D, bn), lambda l, j: (l, 0, j)),
                  pl.BlockSpec((1, 1, bn), lambda l, j: (l, 0, j))],
        out_specs=pl.BlockSpec((1, 8, bn), lambda l, j: (l, 0, j)),
        out_shape=jax.ShapeDtypeStruct((depth, 8, n6), F32),
        compiler_params=_params(2),
        name="ada_mod",
    )(cvec, ada_w, ada_b.reshape(depth, 1, n6))


def _halo_specs(n_rows, tm):
    t8 = tm // HALO_ROWS
    last = n_rows // HALO_ROWS - 1
    return [pl.BlockSpec((HALO_ROWS, D), lambda i: (jnp.maximum(i * t8 - 1, 0), 0)),
            pl.BlockSpec((HALO_ROWS, D), lambda i: (jnp.minimum((i + 1) * t8, last), 0))]


def _mod_spec(n_rows, n_groups, tm):
    rows_per_group = n_rows // n_groups
    return pl.BlockSpec((1, 6, D), lambda i: ((i * tm) // rows_per_group, 0, 0))


class _Units:
    def __init__(self, n_rows, seq, tm):
        assert n_rows % tm == 0 and tm % UNIT == 0
        self.tm, self.seq = tm, seq
        self.n = tm // UNIT
        self.halo = seq > UNIT
        assert seq == UNIT or (seq > tm and seq % tm == 0)
        self.ext_h = UNIT + 2 * HALO_ROWS if self.halo else UNIT
        self.ext_a = UNIT + 2 * HALO_ROWS

    def fill_hext(self, hext_ref, x, xp_ref, xn_ref, g, shift, scale, tile_idx):
        h = _modulate(x, g, shift, scale)
        for u in range(self.n):
            hext_ref[u, 0:UNIT, :] = h[u * UNIT:(u + 1) * UNIT].astype(BF16)
        if not self.halo:
            return
        tiles_per_seq = self.seq // self.tm
        at_start = (tile_idx % tiles_per_seq) == 0
        at_end = (tile_idx % tiles_per_seq) == tiles_per_seq - 1
        for u in range(self.n):
            if u > 0:
                prev = h[u * UNIT - HALO_ROWS:u * UNIT]
            else:
                prev = jnp.where(at_start, 0.0, _modulate(xp_ref[...], g, shift, scale))
            if u < self.n - 1:
                nxt = h[(u + 1) * UNIT:(u + 1) * UNIT + HALO_ROWS]
            else:
                nxt = jnp.where(at_end, 0.0, _modulate(xn_ref[...], g, shift, scale))
            hext_ref[u, UNIT:UNIT + HALO_ROWS, :] = prev.astype(BF16)
            hext_ref[u, UNIT + HALO_ROWS:UNIT + 2 * HALO_ROWS, :] = nxt.astype(BF16)


    def zero_gaps(self, ref):
        zeros = jnp.zeros((HALO_ROWS, LANES), ref.dtype)
        for c in range(ref.shape[0]):
            ref[c, 0:HALO_ROWS, :] = zeros
            ref[c, HALO_ROWS + UNIT:2 * HALO_ROWS + UNIT, :] = zeros

    def store_padded(self, ref, r):
        for c in range(ref.shape[0]):
            cols = slice(c * LANES, (c + 1) * LANES)
            ref[c, HALO_ROWS:HALO_ROWS + UNIT, :] = r[0:UNIT, cols]
            if self.halo:
                ref[c, 0:HALO_ROWS, :] = r[UNIT:UNIT + HALO_ROWS, cols]
                ref[c, HALO_ROWS + UNIT:2 * HALO_ROWS + UNIT, :] = (
                    r[UNIT + HALO_ROWS:UNIT + 2 * HALO_ROWS, cols])

    def conv3(self, ref, w):
        b = HALO_ROWS
        out = []
        for c in range(ref.shape[0]):
            wc = w[:, c * LANES:(c + 1) * LANES]
            out.append(ref[c, b - 1:b - 1 + UNIT, :] * wc[0:1] + ref[c, b:b + UNIT, :] * wc[1:2]
                       + ref[c, b + 1:b + 1 + UNIT, :] * wc[2:3])
        return jnp.concatenate(out, axis=1)


def _ffn_kernel(*refs, units, final):
    halo, n_units = units.halo, units.n
    it = iter(refs)
    x_ref = next(it)
    xp_ref = next(it) if halo else None
    xn_ref = next(it) if halo else None
    mod_ref, g_ref, wup_ref, cw_ref, wdn_ref = (next(it) for _ in range(5))
    fg_ref = next(it) if final else None
    o_ref, hext_ref, acc_ref = (next(it) for _ in range(3))
    a_refs = [[[next(it), next(it)] for _ in range(n_units)] for _ in range(2)]

    x = x_ref[...]
    mod = mod_ref[0]
    units.fill_hext(hext_ref, x, xp_ref, xn_ref, g_ref[...], mod[3:4], mod[4:5], pl.program_id(0))
    if not halo:
        for p in range(2):
            for u in range(n_units):
                for t in range(2):
                    units.zero_gaps(a_refs[p][u][t])

    def up(u, j):
        c0 = j * FF_CHUNK
        hx = hext_ref[u]
        for t, col in enumerate((c0, D_FF + c0)):
            units.store_padded(a_refs[j % 2][u][t], _dot(hx, wup_ref[:, col:col + FF_CHUNK]))

    n_chunks = D_FF // FF_CHUNK
    for u in range(n_units):
        up(u, 0)
    for j in range(n_chunks):
        c0 = j * FF_CHUNK
        p = j % 2
        for u in range(n_units):
            if j + 1 < n_chunks:
                up(u, j + 1)
            cg = units.conv3(a_refs[p][u][0], cw_ref[:, c0:c0 + FF_CHUNK])
            cu = units.conv3(a_refs[p][u][1], cw_ref[:, D_FF + c0:D_FF + c0 + FF_CHUNK])
            act = ((cg * jax.nn.sigmoid(cg)) * cu).astype(BF16)
            part = _dot(act, wdn_ref[c0:c0 + FF_CHUNK, :])
            rows = slice(u * UNIT, (u + 1) * UNIT)
            if j == 0:
                acc_ref[rows, :] = part
            else:
                acc_ref[rows, :] += part

    y = x + mod[5:6] * acc_ref[...]
    if final:
        y = _rms(y, fg_ref[...])
    o_ref[...] = y


def _layer_resident(arr, layer):
    nd = arr.ndim - 1
    return pl.BlockSpec((None,) + arr.shape[1:], lambda *_: (layer,) + (0,) * nd,
                        pipeline_mode=pl.Buffered(1))


def _ffn(x, mod, g, wup, cw, wdn, final_g, *, layer, seq, tm):
    n = x.shape[0]
    units = _Units(n, seq, tm)
    halo = units.halo
    final = final_g is not None
    in_specs = [pl.BlockSpec((tm, D), lambda i: (i, 0))]
    args = [x]
    if halo:
        in_specs += _halo_specs(n, tm)
        args += [x, x]
    in_specs += [_mod_spec(n, mod.shape[0], tm), _layer_resident(g, layer),
                 _layer_resident(wup, layer), _layer_resident(cw, layer),
                 _layer_resident(wdn, layer)]
    args += [mod, g, wup, cw, wdn]
    if final:
        in_specs.append(_resident((1, D)))
        args.append(final_g)
    scratch = [pltpu.VMEM((units.n, units.ext_h, D), BF16), pltpu.VMEM((tm, D), F32)]
    scratch += [pltpu.VMEM((FF_CHUNK // LANES, units.ext_a, LANES), F32)] * (4 * units.n)
    return pl.pallas_call(
        functools.partial(_ffn_kernel, units=units, final=final),
        grid=(n // tm,),
        in_specs=in_specs,
        out_specs=pl.BlockSpec((tm, D), lambda i: (i, 0)),
        out_shape=jax.ShapeDtypeStruct((n, D), F32),
        scratch_shapes=scratch,
        compiler_params=_params(1),
        name="conv_ffn",
    )(*args)


class _Stage(NamedTuple):
    body: Callable
    param_specs: list
    params: list
    scratch: list


def _two_stage_body(*refs, first, second, n_first, n_second, n_first_scratch):
    x_ref, mod_ref = refs[0], refs[1]
    p1 = refs[2:2 + n_first]
    p2 = refs[2 + n_first:2 + n_first + n_second]
    o_ref = refs[2 + n_first + n_second]
    rest = refs[3 + n_first + n_second:]
    mid_ref = rest[n_first_scratch]
    first(x_ref, mod_ref, *p1, mid_ref, *rest[:n_first_scratch])
    second(mid_ref, mod_ref, *p2, o_ref, *rest[n_first_scratch + 1:])


def _run_stages(x, mod, stages, *, tm, name):
    n = x.shape[0]
    in_specs = [pl.BlockSpec((tm, D), lambda i: (i, 0)), _mod_spec(n, mod.shape[0], tm)]
    args = [x, mod]
    for st in stages:
        in_specs += st.param_specs
        args += st.params
    if len(stages) == 1:
        body, scratch = stages[0].body, stages[0].scratch
    else:
        first, second = stages
        body = functools.partial(_two_stage_body, first=first.body, second=second.body,
                                 n_first=len(first.params), n_second=len(second.params),
                                 n_first_scratch=len(first.scratch))
        scratch = first.scratch + [pltpu.VMEM((tm, D), F32)] + second.scratch
    return pl.pallas_call(
        body,
        grid=(n // tm,),
        in_specs=in_specs,
        out_specs=pl.BlockSpec((tm, D), lambda i: (i, 0)),
        out_shape=jax.ShapeDtypeStruct((n, D), F32),
        scratch_shapes=scratch,
        compiler_params=_params(1),
        name=name,
    )(*args)


def _ffn_stage(n, g, wup, cw, wdn, final_g, *, layer, seq, tm):
    units = _Units(n, seq, tm)
    assert not units.halo
    final = final_g is not None
    specs = [_layer_resident(g, layer), _layer_resident(wup, layer), _layer_resident(cw, layer),
             _layer_resident(wdn, layer)]
    params = [g, wup, cw, wdn]
    if final:
        specs.append(_resident((1, D)))
        params.append(final_g)
    scratch = [pltpu.VMEM((units.n, units.ext_h, D), BF16), pltpu.VMEM((tm, D), F32)]
    scratch += [pltpu.VMEM((FF_CHUNK // LANES, units.ext_a, LANES), F32)] * (4 * units.n)
    return _Stage(functools.partial(_ffn_kernel, units=units, final=final), specs, params, scratch)


def _sconv_kernel(*refs, units):
    halo, n_units = units.halo, units.n
    it = iter(refs)
    x_ref = next(it)
    xp_ref = next(it) if halo else None
    xn_ref = next(it) if halo else None
    mod_ref, g_ref, win_ref, cw_ref, wout_ref, o_ref, hext_ref, t_ref = (next(it) for _ in range(8))
    p_refs = [[next(it) for _ in range(n_units)] for _ in range(2)]

    x = x_ref[...]
    mod = mod_ref[0]
    units.fill_hext(hext_ref, x, xp_ref, xn_ref, g_ref[...], mod[0:1], mod[1:2], pl.program_id(0))
    if not halo:
        for p in range(2):
            for u in range(n_units):
                units.zero_gaps(p_refs[p][u])

    for j in range(D // FF_CHUNK):
        c0 = j * FF_CHUNK
        for u in range(n_units):
            hx = hext_ref[u]
            b = _dot(hx, win_ref[:, c0:c0 + FF_CHUNK])
            cg = _dot(hx, win_ref[:, D + c0:D + c0 + FF_CHUNK])
            xin = _dot(hx, win_ref[:, 2 * D + c0:2 * D + c0 + FF_CHUNK])
            units.store_padded(p_refs[j % 2][u], cg * xin)
            y = units.conv3(p_refs[j % 2][u], cw_ref[:, c0:c0 + FF_CHUNK])
            t_ref[u * UNIT:(u + 1) * UNIT, c0:c0 + FF_CHUNK] = (b[0:UNIT] * y).astype(BF16)

    o_ref[...] = x + mod[2:3] * _dot(t_ref[...], wout_ref[...])


def _sconv_stage(n, g, win, cw, wout, *, seq, tm):
    units = _Units(n, seq, tm)
    assert not units.halo
    scratch = [pltpu.VMEM((units.n, units.ext_h, D), BF16), pltpu.VMEM((tm, D), BF16)]
    scratch += [pltpu.VMEM((FF_CHUNK // LANES, units.ext_a, LANES), F32)] * (2 * units.n)
    return _Stage(functools.partial(_sconv_kernel, units=units),
                  [_resident((1, D)), _resident(win.shape), _resident(cw.shape),
                   _resident(wout.shape)],
                  [g, win, cw, wout], scratch)


def _sconv(x, mod, g, win, cw, wout, *, seq, tm):
    n = x.shape[0]
    units = _Units(n, seq, tm)
    halo = units.halo
    in_specs = [pl.BlockSpec((tm, D), lambda i: (i, 0))]
    args = [x]
    if halo:
        in_specs += _halo_specs(n, tm)
        args += [x, x]
    in_specs += [_mod_spec(n, mod.shape[0], tm), _resident((1, D)), _resident(win.shape),
                 _resident(cw.shape), _resident(wout.shape)]
    args += [mod, g, win, cw, wout]
    scratch = [pltpu.VMEM((units.n, units.ext_h, D), BF16), pltpu.VMEM((tm, D), BF16)]
    scratch += [pltpu.VMEM((FF_CHUNK // LANES, units.ext_a, LANES), F32)] * (2 * units.n)
    return pl.pallas_call(
        functools.partial(_sconv_kernel, units=units),
        grid=(n // tm,),
        in_specs=in_specs,
        out_specs=pl.BlockSpec((tm, D), lambda i: (i, 0)),
        out_shape=jax.ShapeDtypeStruct((n, D), F32),
        scratch_shapes=scratch,
        compiler_params=_params(1),
        name="short_conv",
    )(*args)


def _gelu_tanh(x):
    return 0.5 * x * (1.0 + jnp.tanh(math.sqrt(2.0 / math.pi) * (x + 0.044715 * (x * x * x))))


def _sgu_kernel(x_ref, mod_ref, g_ref, win_ref, lng_ref, ws_ref, bs_ref, wout_ref, o_ref,
                v_ref, t_ref, acc_ref, *, tm):
    x = x_ref[...]
    mod = mod_ref[0]
    h = _modulate(x, g_ref[...], mod[0:1], mod[1:2]).astype(BF16)
    v = _gelu_tanh(_dot(h, win_ref[:, D:2 * D]))
    mu = jnp.mean(v, axis=-1, keepdims=True)
    vc = v - mu
    var = jnp.mean(vc * vc, axis=-1, keepdims=True)
    v_ref[...] = ((vc * lax.rsqrt(var + EPS)) * lng_ref[...]).astype(BF16)

    n_chunks = tm // SGU_CHUNK
    n_pairs = SGU_GROUPS // 2
    pair_w = 2 * LANES

    def gate_proj(p):
        return _dot(h, win_ref[:, p * pair_w:(p + 1) * pair_w])

    u_next = gate_proj(0)
    for p in range(n_pairs):
        u_pair = _gelu_tanh(u_next)
        if p + 1 < n_pairs:
            u_next = gate_proj(p + 1)
        for gi in (2 * p, 2 * p + 1):
            cols = slice(gi * LANES, (gi + 1) * LANES)
            u = u_pair[:, (gi % 2) * LANES:(gi % 2 + 1) * LANES]
            rhs = jnp.concatenate(
                [v_ref[n * SGU_CHUNK:(n + 1) * SGU_CHUNK, cols] for n in range(n_chunks)], axis=1)
            mixed = _dot(ws_ref[gi], rhs)
            bias = bs_ref[:, cols]
            for n in range(n_chunks):
                rows = slice(n * SGU_CHUNK, (n + 1) * SGU_CHUNK)
                t_ref[rows, cols] = (
                    u[rows] * (mixed[:, n * LANES:(n + 1) * LANES] + bias)).astype(BF16)
        pcols = slice(p * pair_w, (p + 1) * pair_w)
        part = _dot(t_ref[:, pcols], wout_ref[pcols, :])
        if p == 0:
            acc_ref[...] = part
        else:
            acc_ref[...] += part

    o_ref[...] = x + mod[2:3] * acc_ref[...]


def _sgu_stage(g, win, lng, ws, bs_full, wout, *, tm):
    return _Stage(functools.partial(_sgu_kernel, tm=tm),
                  [_resident((1, D)), _resident(win.shape), _resident((1, D)), _resident(ws.shape),
                   _resident(bs_full.shape), _resident(wout.shape)],
                  [g, win, lng, ws, bs_full, wout],
                  [pltpu.VMEM((tm, D), BF16), pltpu.VMEM((tm, D), BF16), pltpu.VMEM((tm, D), F32)])


LOG2E = math.log2(math.e)
Q_SCALE = HD ** -0.5 * LOG2E


def _softmax_pv_t(parts, sink, v_t):
    sink2 = sink * LOG2E
    m = sink2
    for s_t in parts:
        m = jnp.maximum(m, jnp.max(s_t, axis=0, keepdims=True))
    den = jnp.exp2(sink2 - m)
    ps = []
    for s_t in parts:
        p = jnp.exp2(s_t - m)
        den = den + jnp.sum(p, axis=0, keepdims=True)
        ps.append(p.astype(BF16))
    p_all = ps[0] if len(ps) == 1 else jnp.concatenate(ps, axis=0)
    return _dot(v_t, p_all) / den


def _attn_prompt_kernel(x_ref, mod_ref, g_ref, wqkv_ref, wo_ref, sink_ref, o_ref, k_ref, v_ref,
                        q_s, k_s, ot_s, o_s, *, bb, seq):
    x = x_ref[...]
    mod = mod_ref[0]
    h = _modulate(x, g_ref[...], mod[0:1], mod[1:2]).astype(BF16)
    nq = N_HEADS * HD
    q_s[...] = (_dot(h, wqkv_ref[:, 0:nq]) * Q_SCALE).astype(BF16)
    k = _dot(h, wqkv_ref[:, nq:nq + KV_DIM])
    v = _dot(h, wqkv_ref[:, nq + KV_DIM:nq + 2 * KV_DIM])
    k_ref[...] = k
    v_ref[...] = v
    k_s[...] = k.astype(BF16)

    def one_batch(b, carry):
        rows = pl.ds(pl.multiple_of(b * seq, seq), seq)
        v_t = v_ref[rows, :].T.astype(BF16)
        for kv in range(N_KV):
            kh = k_s[rows, kv * HD:(kv + 1) * HD]
            vh_t = v_t[kv * HD:(kv + 1) * HD, :]
            scores = [_dot_nt(kh, q_s[rows, hd * HD:(hd + 1) * HD])
                      for hd in range(kv * Q_PER_KV, (kv + 1) * Q_PER_KV)]
            for j, s_t in enumerate(scores):
                hd = kv * Q_PER_KV + j
                ot_s[hd * HD:(hd + 1) * HD, :] = _softmax_pv_t([s_t], sink_ref[0, hd], vh_t)
        o_s[rows, :] = ot_s[...].T.astype(BF16)
        return carry

    lax.fori_loop(0, bb, one_batch, 0)
    o_ref[...] = x + mod[2:3] * _dot(o_s[...], wo_ref[...])


def _attn_prompt(x, mod, g, wqkv, wo, sink, *, seq, bb):
    n = x.shape[0]
    tm = bb * seq
    nq = N_HEADS * HD
    return pl.pallas_call(
        functools.partial(_attn_prompt_kernel, bb=bb, seq=seq),
        grid=(n // tm,),
        in_specs=[pl.BlockSpec((tm, D), lambda i: (i, 0)), _mod_spec(n, mod.shape[0], tm),
                  _resident((1, D)), _resident(wqkv.shape), _resident(wo.shape),
                  pl.BlockSpec(memory_space=pltpu.SMEM)],
        out_specs=[pl.BlockSpec((tm, D), lambda i: (i, 0)),
                   pl.BlockSpec((tm, KV_DIM), lambda i: (i, 0)),
                   pl.BlockSpec((tm, KV_DIM), lambda i: (i, 0))],
        out_shape=[jax.ShapeDtypeStruct((n, D), F32),
                   jax.ShapeDtypeStruct((n, KV_DIM), F32),
                   jax.ShapeDtypeStruct((n, KV_DIM), F32)],
        scratch_shapes=[pltpu.VMEM((tm, nq), BF16), pltpu.VMEM((tm, KV_DIM), BF16),
                        pltpu.VMEM((nq, seq), F32), pltpu.VMEM((tm, nq), BF16)],
        compiler_params=_params(1),
        name="attn_prompt",
    )(x, mod, g, wqkv, wo, sink)


def _rope_tables(seq):
    t = jnp.arange(seq)
    half = HD // 4
    inv = ROPE_THETA ** (-jnp.arange(half, dtype=F32) / half)
    ang_row = (t // GRID_W).astype(F32)[:, None] * inv[None, :]
    ang_col = (t % GRID_W).astype(F32)[:, None] * inv[None, :]
    cos_h = jnp.concatenate([jnp.cos(ang_row)] * 2 + [jnp.cos(ang_col)] * 2, axis=-1)
    sin_h = jnp.concatenate([-jnp.sin(ang_row), jnp.sin(ang_row),
                             -jnp.sin(ang_col), jnp.sin(ang_col)], axis=-1)
    reps = LANES // HD
    return jnp.tile(cos_h, (1, reps)), jnp.tile(sin_h, (1, reps))


def _qkv_rope_kernel(x_ref, mod_ref, g_ref, wqkv_ref, cos_ref, sin_ref, q_ref, k_ref, v_ref):
    x = x_ref[...]
    mod = mod_ref[0]
    h = _modulate(x, g_ref[...], mod[0:1], mod[1:2]).astype(BF16)
    nq = N_HEADS * HD
    cos = cos_ref[...]
    sin = sin_ref[...]
    lane = lax.broadcasted_iota(jnp.int32, cos.shape, 1)
    low = (lane % (HD // 2)) < (HD // 4)
    for blk in range((nq + KV_DIM) // LANES):
        c0 = blk * LANES
        if blk % 2 == 0:
            pair = _dot(h, wqkv_ref[:, c0:c0 + 2 * LANES])
        a = pair[:, (blk % 2) * LANES:(blk % 2 + 1) * LANES]
        partner = jnp.where(low, pltpu.roll(a, LANES - HD // 4, 1), pltpu.roll(a, HD // 4, 1))
        r = a * cos + partner * sin
        if c0 < nq:
            q_ref[:, c0:c0 + LANES] = (r * Q_SCALE).astype(BF16)
        else:
            k_ref[:, c0 - nq:c0 - nq + LANES] = r.astype(BF16)
    v_ref[...] = _dot(h, wqkv_ref[:, nq + KV_DIM:nq + 2 * KV_DIM]).astype(BF16)


def _qkv_rope(x, mod, g, wqkv, cos, sin, *, seq, tm):
    n = x.shape[0]
    nq = N_HEADS * HD
    tiles_per_seq = seq // tm
    return pl.pallas_call(
        _qkv_rope_kernel,
        grid=(n // tm,),
        in_specs=[pl.BlockSpec((tm, D), lambda i: (i, 0)), _mod_spec(n, mod.shape[0], tm),
                  _resident((1, D)), _resident(wqkv.shape),
                  pl.BlockSpec((tm, LANES), lambda i: (i % tiles_per_seq, 0)),
                  pl.BlockSpec((tm, LANES), lambda i: (i % tiles_per_seq, 0))],
        out_specs=[pl.BlockSpec((tm, nq), lambda i: (i, 0)),
                   pl.BlockSpec((tm, KV_DIM), lambda i: (i, 0)),
                   pl.BlockSpec((tm, KV_DIM), lambda i: (i, 0))],
        out_shape=[jax.ShapeDtypeStruct((n, nq), BF16),
                   jax.ShapeDtypeStruct((n, KV_DIM), BF16),
                   jax.ShapeDtypeStruct((n, KV_DIM), BF16)],
        compiler_params=_params(1),
        name="qkv_rope",
    )(x, mod, g, wqkv, cos, sin)


def _attn_latent_kernel(x_ref, mod_ref, q_ref, k_ref, v_ref, ck_ref, cv_ref, wo_ref, sink_ref,
                        o_ref, ot_s, *, tq, seq, span):
    qt = pl.program_id(1)
    ws = pl.multiple_of(jnp.clip(qt * tq - WINDOW, 0, seq - span), WINDOW)
    past = ck_ref.shape[1]
    kpos = ws + lax.broadcasted_iota(jnp.int32, (span, tq), 0)
    qpos = qt * tq + lax.broadcasted_iota(jnp.int32, (span, tq), 1)
    valid = jnp.abs(qpos - kpos) <= WINDOW
    v_t = jnp.concatenate([v_ref[0, pl.ds(ws, span), :].astype(F32).T,
                           cv_ref[0].astype(F32).T], axis=1).astype(BF16)

    for kv in range(N_KV):
        lanes = slice(kv * HD, (kv + 1) * HD)
        kcat = jnp.concatenate([k_ref[0, pl.ds(ws, span), lanes], ck_ref[0, :, lanes]], axis=0)
        scores = [_dot_nt(kcat, q_ref[0, :, hd * HD:(hd + 1) * HD])
                  for hd in range(kv * Q_PER_KV, (kv + 1) * Q_PER_KV)]
        for j, s_t in enumerate(scores):
            hd = kv * Q_PER_KV + j
            parts = [jnp.where(valid, s_t[0:span], NEG_BIG), s_t[span:span + past]]
            ot_s[hd * HD:(hd + 1) * HD, :] = _softmax_pv_t(parts, sink_ref[0, hd], v_t[lanes, :])

    mod = mod_ref[0]
    o_ref[0] = x_ref[0] + mod[2:3] * _dot(ot_s[...].T.astype(BF16), wo_ref[...])


def _attn_latent(x, mod, q, k, v, ck, cv, wo, sink, *, tq):
    nb, seq, _ = x.shape
    nq = N_HEADS * HD
    past = ck.shape[1]
    span = tq + 2 * WINDOW
    return pl.pallas_call(
        functools.partial(_attn_latent_kernel, tq=tq, seq=seq, span=span),
        grid=(nb, seq // tq),
        in_specs=[pl.BlockSpec((1, tq, D), lambda b, t: (b, t, 0)),
                  pl.BlockSpec((1, 6, D), lambda b, t: (b, 0, 0)),
                  pl.BlockSpec((1, tq, nq), lambda b, t: (b, t, 0)),
                  pl.BlockSpec((1, seq, KV_DIM), lambda b, t: (b, 0, 0)),
                  pl.BlockSpec((1, seq, KV_DIM), lambda b, t: (b, 0, 0)),
                  pl.BlockSpec((1, past, KV_DIM), lambda b, t: (b, 0, 0)),
                  pl.BlockSpec((1, past, KV_DIM), lambda b, t: (b, 0, 0)),
                  _resident(wo.shape),
                  pl.BlockSpec(memory_space=pltpu.SMEM)],
        out_specs=pl.BlockSpec((1, tq, D), lambda b, t: (b, t, 0)),
        out_shape=jax.ShapeDtypeStruct((nb, seq, D), F32),
        scratch_shapes=[pltpu.VMEM((nq, tq), F32)],
        compiler_params=_params(2),
        name="attn_latent",
    )(x, mod, q, k, v, ck, cv, wo, sink)


def _dft_angle(k, t, period):
    return ((k * t) % period).astype(F32) * (2.0 * math.pi / period)


def _fourier_tables(seq):
    c = jnp.arange(FOURIER_GROUP_DIM, dtype=jnp.int32)
    ang_c = _dft_angle(c[:, None], c[None, :], FOURIER_GROUP_DIM)
    chan = jnp.concatenate([jnp.cos(ang_c), jnp.sin(ang_c)], axis=1).astype(BF16)
    k = jnp.arange(seq, dtype=jnp.int32)
    hi = jnp.arange(seq // LANES, dtype=jnp.int32) * LANES
    lo = jnp.arange(LANES, dtype=jnp.int32)
    ang_hi = _dft_angle(k[:, None], hi[None, :], seq)
    ang_lo = _dft_angle(k[:, None], lo[None, :], seq)
    return chan, jnp.cos(ang_hi), jnp.sin(ang_hi), jnp.cos(ang_lo), jnp.sin(ang_lo)


def _fourier_chan_kernel(x_ref, mod_ref, g_ref, chan_ref, o_ref, *, bb, tt):
    x = x_ref[...].reshape(bb * tt, D)
    mod = mod_ref[0]
    h = _modulate(x, g_ref[...], mod[0:1], mod[1:2]).astype(BF16)
    for gi in range(D // FOURIER_GROUP_DIM):
        cols = slice(gi * LANES, (gi + 1) * LANES)
        cs = _dot(h[:, cols], chan_ref[...]).astype(BF16)
        for b in range(bb):
            rows = slice(b * tt, (b + 1) * tt)
            o_ref[b, 0, :, cols] = cs[rows, 0:LANES]
            o_ref[b, 1, :, cols] = cs[rows, LANES:2 * LANES]


def _fourier_chan(x, mod, g, chan, *, bb, tt):
    nb, seq, _ = x.shape
    n_groups = mod.shape[0]
    return pl.pallas_call(
        functools.partial(_fourier_chan_kernel, bb=bb, tt=tt),
        grid=(nb // bb, seq // tt),
        in_specs=[pl.BlockSpec((bb, tt, D), lambda b, t: (b, t, 0)),
                  pl.BlockSpec((1, 6, D), lambda b, t: ((b * bb * n_groups) // nb, 0, 0)),
                  _resident((1, D)), _resident(chan.shape)],
        out_specs=pl.BlockSpec((bb, 2, tt, D), lambda b, t: (b, 0, t, 0)),
        out_shape=jax.ShapeDtypeStruct((nb, 2, seq, D), BF16),
        compiler_params=_params(2),
        name="fourier_chan",
    )(x, mod, g, chan)


def _fourier_pos_kernel(x_ref, mod_ref, hh_ref, chi_ref, shi_ref, clo_ref, slo_ref, wout_ref,
                        o_ref, lhs_ref, *, seq, scale):
    clo = clo_ref[...]
    slo = slo_ref[...]
    n_hi = seq // LANES
    blk = min(n_hi, DFT_BLOCK // LANES)
    f = None
    for a0 in range(0, n_hi, blk):
        for a in range(a0, a0 + blk):
            ca = chi_ref[:, a:a + 1]
            sa = shi_ref[:, a:a + 1]
            lhs_ref[:, a * LANES:(a + 1) * LANES] = (ca * clo - sa * slo).astype(BF16)
            lhs_ref[:, seq + a * LANES:seq + (a + 1) * LANES] = (
                -(sa * clo + ca * slo)).astype(BF16)
        for base in (0, seq):
            cols = slice(base + a0 * LANES, base + (a0 + blk) * LANES)
            part = _dot(lhs_ref[:, cols], hh_ref[0, cols, :])
            f = part if f is None else f + part
    f = (f * scale).astype(BF16)
    mod = mod_ref[0]
    o_ref[0] = x_ref[0] + mod[2:3] * _dot(f, wout_ref[...])


def _fourier_pos(x, mod, hh, chi, shi, clo, slo, wout, *, tr):
    nb, seq, _ = x.shape
    n_groups = mod.shape[0]
    n_hi = seq // LANES
    scale = 1.0 / math.sqrt(seq * FOURIER_GROUP_DIM)
    return pl.pallas_call(
        functools.partial(_fourier_pos_kernel, seq=seq, scale=scale),
        grid=(nb, seq // tr),
        in_specs=[pl.BlockSpec((1, tr, D), lambda b, t: (b, t, 0)),
                  pl.BlockSpec((1, 6, D), lambda b, t: ((b * n_groups) // nb, 0, 0)),
                  pl.BlockSpec((1, 2 * seq, D), lambda b, t: (b, 0, 0)),
                  pl.BlockSpec((tr, n_hi), lambda b, t: (t, 0)),
                  pl.BlockSpec((tr, n_hi), lambda b, t: (t, 0)),
                  pl.BlockSpec((tr, LANES), lambda b, t: (t, 0)),
                  pl.BlockSpec((tr, LANES), lambda b, t: (t, 0)),
                  _resident(wout.shape)],
        out_specs=pl.BlockSpec((1, tr, D), lambda b, t: (b, t, 0)),
        out_shape=jax.ShapeDtypeStruct((nb, seq, D), F32),
        scratch_shapes=[pltpu.VMEM((tr, 2 * seq), BF16)],
        compiler_params=_params(2),
        name="fourier_pos",
    )(x, mod, hh, chi, shi, clo, slo, wout)


def _fourier(x, mod, g, wout, *, bb, tt, tr):
    nb, seq, _ = x.shape
    chan, chi, shi, clo, slo = _fourier_tables(seq)
    hh = _fourier_chan(x, mod, g, chan, bb=bb, tt=tt)
    return _fourier_pos(x, mod, hh.reshape(nb, 2 * seq, D), chi, shi, clo, slo, wout, tr=tr)


def _fourier_short_kernel(x_ref, mod_ref, g_ref, chan_ref, chi_ref, shi_ref, clo_ref, slo_ref,
                          wout_ref, o_ref, lhs_ref, hh_ref, f_ref, *, bb, seq, scale):
    @pl.when(pl.program_id(0) == 0)
    def _():
        clo = clo_ref[...]
        slo = slo_ref[...]
        for a in range(seq // LANES):
            ca = chi_ref[:, a:a + 1]
            sa = shi_ref[:, a:a + 1]
            lhs_ref[:, a * LANES:(a + 1) * LANES] = (ca * clo - sa * slo).astype(BF16)
            lhs_ref[:, seq + a * LANES:seq + (a + 1) * LANES] = (-(sa * clo + ca * slo)).astype(BF16)

    x = x_ref[...]
    mod = mod_ref[0]
    h = _modulate(x, g_ref[...], mod[0:1], mod[1:2]).astype(BF16)
    for gi in range(D // FOURIER_GROUP_DIM):
        cols = slice(gi * LANES, (gi + 1) * LANES)
        cs = _dot(h[:, cols], chan_ref[...]).astype(BF16)
        for b in range(bb):
            rows = slice(b * seq, (b + 1) * seq)
            hh_ref[b, 0:seq, cols] = cs[rows, 0:LANES]
            hh_ref[b, seq:2 * seq, cols] = cs[rows, LANES:2 * LANES]
    for b in range(bb):
        f_ref[b * seq:(b + 1) * seq, :] = (_dot(lhs_ref[...], hh_ref[b]) * scale).astype(BF16)
    o_ref[...] = x + mod[2:3] * _dot(f_ref[...], wout_ref[...])


def _fourier_short_stage(g, wout, *, seq, tm):
    bb = tm // seq
    chan, chi, shi, clo, slo = _fourier_tables(seq)
    scale = 1.0 / math.sqrt(seq * FOURIER_GROUP_DIM)
    params = [g, chan, chi, shi, clo, slo, wout]
    return _Stage(functools.partial(_fourier_short_kernel, bb=bb, seq=seq, scale=scale),
                  [_resident(p.shape) for p in params], params,
                  [pltpu.VMEM((seq, 2 * seq), BF16), pltpu.VMEM((bb, 2 * seq, D), BF16),
                   pltpu.VMEM((tm, D), BF16)])


def kernel(x_prompt, x_sample, cache_k, cache_v, c, c_ctx, ada_w, ada_b, norm_mix_g, norm_ffn_g,
           final_g, attn_wqkv, attn_wo, attn_sink, sgu_w_in, sgu_ln_g, sgu_w_s, sgu_b_s,
           sgu_w_out, sc_w_in, sc_conv, sc_w_out, fn_w_out, ffn_w_up, ffn_conv, ffn_w_down):
    nbp, seq_p, _ = x_prompt.shape
    nbs, seq_s, _ = x_sample.shape
    depth = ada_w.shape[0]
    tm = 512

    cvec = jnp.concatenate([c_ctx[None, :], c, jnp.zeros((8 - 1 - nbs, D), F32)], axis=0)
    mods = _ada_mod(cvec, ada_w, ada_b).reshape(depth, 8, 6, D)

    xp = x_prompt.reshape(nbp * seq_p, D)
    xs = x_sample.reshape(nbs * seq_s, D)
    new_k = new_v = None
    wup = ffn_w_up.astype(BF16)
    wdn = ffn_w_down.astype(BF16)
    g_ffn = norm_ffn_g[:, None, :]

    for l in range(depth):
        kind, j = l % 4, l // 4
        mp = mods[l, 0:1]
        ms = mods[l, 1:1 + nbs]
        g_mix = norm_mix_g[l][None, :]
        fg = final_g[None, :] if l == depth - 1 else None
        ffn_p = _ffn_stage(xp.shape[0], g_ffn, wup, ffn_conv, wdn, fg, layer=l, seq=seq_p, tm=tm)
        if kind == 0:
            wqkv = attn_wqkv[j].astype(BF16)
            wo = attn_wo[j].astype(BF16)
            sink = attn_sink[j][None, :]
            xp, new_k, new_v = _attn_prompt(xp, mp, g_mix, wqkv, wo, sink, seq=seq_p, bb=4)
            xp = _run_stages(xp, mp, [ffn_p], tm=tm, name="conv_ffn")
            cos, sin = _rope_tables(seq_s)
            q, k, v = _qkv_rope(xs, ms, g_mix, wqkv, cos, sin, seq=seq_s, tm=tm)
            past = cache_k.shape[2]
            ck = cache_k[:, j].reshape(nbs, past, KV_DIM).astype(BF16)
            cv = cache_v[:, j].reshape(nbs, past, KV_DIM).astype(BF16)
            xs = _attn_latent(xs.reshape(nbs, seq_s, D), ms, q.reshape(nbs, seq_s, N_HEADS * HD),
                              k.reshape(nbs, seq_s, KV_DIM), v.reshape(nbs, seq_s, KV_DIM),
                              ck, cv, wo, sink, tq=256).reshape(nbs * seq_s, D)
        elif kind == 1:
            win = sgu_w_in[j].astype(BF16)
            ws = sgu_w_s[j].astype(BF16)
            wout = sgu_w_out[j].astype(BF16)
            lng = sgu_ln_g[j][None, :]
            bs_full = jnp.repeat(sgu_b_s[j].T, LANES, axis=1)
            sgu = _sgu_stage(g_mix, win, lng, ws, bs_full, wout, tm=tm)
            xp = _run_stages(xp, mp, [sgu, ffn_p], tm=tm, name="sgu_ffn")
            xs = _run_stages(xs, ms, [sgu], tm=tm, name="sgu")
        elif kind == 2:
            win = sc_w_in[j].astype(BF16)
            wout = sc_w_out[j].astype(BF16)
            sconv_p = _sconv_stage(xp.shape[0], g_mix, win, sc_conv[j], wout, seq=seq_p, tm=tm)
            xp = _run_stages(xp, mp, [sconv_p, ffn_p], tm=tm, name="short_conv_ffn")
            xs = _sconv(xs, ms, g_mix, win, sc_conv[j], wout, seq=seq_s, tm=tm)
        else:
            wout = fn_w_out[j].astype(BF16)
            fnet_p = _fourier_short_stage(g_mix, wout, seq=seq_p, tm=tm)
            xp = _run_stages(xp, mp, [fnet_p, ffn_p], tm=tm, name="fourier_ffn")
            xs = _fourier(xs.reshape(nbs, seq_s, D), ms, g_mix, wout,
                          bb=1, tt=tm, tr=tm).reshape(nbs * seq_s, D)

        xs = _ffn(xs, ms, g_ffn, wup, ffn_conv, wdn, fg, layer=l, seq=seq_s, tm=tm)

    y_prompt = xp.reshape(nbp, seq_p, D)
    y_sample = xs.reshape(nbs, seq_s, D)
    new_cache_k = new_k.reshape(nbp, 1, seq_p, N_KV, HD)
    new_cache_v = new_v.reshape(nbp, 1, seq_p, N_KV, HD)
    return (y_prompt, y_sample, new_cache_k, new_cache_v)
```

```python
import functools
import math
from typing import Callable, NamedTuple

import jax
import jax.numpy as jnp
from jax import lax
from jax.experimental import pallas as pl
from jax.experimental.pallas import tpu as pltpu

D = 1024
N_HEADS = 16
N_KV = 4
HD = 64
Q_PER_KV = N_HEADS // N_KV
KV_DIM = N_KV * HD
WINDOW = 128
GRID_W = 64
ROPE_THETA = 10000.0
SGU_CHUNK = 128
SGU_GROUPS = 8
FOURIER_GROUP_DIM = 128
D_FF = 2816
EPS = 1e-6
NEG_BIG = -1e30

LANES = 128
SUBLANES = 8
HALO_ROWS = SUBLANES
TOKEN_TILE = 512
QUERY_TILE = 256
PROMPT_SEQS_PER_STEP = 4
FF_CHUNK = 256
UNIT = 256
DFT_BLOCK = 512
VMEM_LIMIT = 52 * 1024 * 1024

F32 = jnp.float32
BF16 = jnp.bfloat16


def _resident(shape):
    nd = len(shape)
    return pl.BlockSpec(shape, lambda *_: (0,) * nd, pipeline_mode=pl.Buffered(1))


def _params(n_axes, flags=None):
    return pltpu.CompilerParams(dimension_semantics=("arbitrary",) * n_axes,
                                vmem_limit_bytes=VMEM_LIMIT, flags=flags)


def _modulate(x, g, shift, scale):
    ms = jnp.mean(x * x, axis=-1, keepdims=True)
    return (x * lax.rsqrt(ms + EPS)) * (g * (1.0 + scale)) + shift


def _rms(x, g):
    ms = jnp.mean(x * x, axis=-1, keepdims=True)
    return (x * lax.rsqrt(ms + EPS)) * g


def _dot(a, b):
    return jnp.dot(a, b, preferred_element_type=F32)


def _dot_nt(a, b):
    return lax.dot_general(a, b, (((1,), (1,)), ((), ())), preferred_element_type=F32)


def _ada_kernel(c_ref, w_ref, b_ref, o_ref):
    c = c_ref[...]
    s = (c * jax.nn.sigmoid(c)).astype(BF16)
    o_ref[0] = _dot(s, w_ref[0].astype(BF16)) + b_ref[0]


def _ada_mod(cvec, ada_w, ada_b):
    depth, _, n6 = ada_w.shape
    bn = 2 * D
    return pl.pallas_call(
        _ada_kernel,
        grid=(depth, n6 // bn),
        in_specs=[pl.BlockSpec((SUBLANES, D), lambda l, j: (0, 0)),
                  pl.BlockSpec((1, D, bn), lambda l, j: (l, 0, j)),
                  pl.BlockSpec((1, 1, bn), lambda l, j: (l, 0, j))],
        out_specs=pl.BlockSpec((1, SUBLANES, bn), lambda l, j: (l, 0, j)),
        out_shape=jax.ShapeDtypeStruct((depth, SUBLANES, n6), F32),
        compiler_params=_params(2),
        name="ada_mod",
    )(cvec, ada_w, ada_b.reshape(depth, 1, n6))


def _halo_specs(n_rows, tm):
    t8 = tm // HALO_ROWS
    last = n_rows // HALO_ROWS - 1
    return [pl.BlockSpec((HALO_ROWS, D), lambda i: (jnp.maximum(i * t8 - 1, 0), 0)),
            pl.BlockSpec((HALO_ROWS, D), lambda i: (jnp.minimum((i + 1) * t8, last), 0))]


def _mod_spec(n_rows, n_groups, tm):
    rows_per_group = n_rows // n_groups
    return pl.BlockSpec((1, 6, D), lambda i: ((i * tm) // rows_per_group, 0, 0))


class _Units:
    def __init__(self, n_rows, seq, tm):
        assert n_rows % tm == 0 and tm % UNIT == 0
        self.tm, self.seq = tm, seq
        self.n = tm // UNIT
        self.halo = seq > UNIT
        assert seq == UNIT or (seq > tm and seq % tm == 0)
        self.ext_h = UNIT + 2 * HALO_ROWS if self.halo else UNIT
        self.ext_a = UNIT + 2 * HALO_ROWS

    def fill_hext(self, hext_ref, x, xp_ref, xn_ref, g, shift, scale, tile_idx):
        h = _modulate(x, g, shift, scale)
        for u in range(self.n):
            hext_ref[u, 0:UNIT, :] = h[u * UNIT:(u + 1) * UNIT].astype(BF16)
        if not self.halo:
            return
        tiles_per_seq = self.seq // self.tm
        at_start = (tile_idx % tiles_per_seq) == 0
        at_end = (tile_idx % tiles_per_seq) == tiles_per_seq - 1
        for u in range(self.n):
            if u > 0:
                prev = h[u * UNIT - HALO_ROWS:u * UNIT]
            else:
                prev = jnp.where(at_start, 0.0, _modulate(xp_ref[...], g, shift, scale))
            if u < self.n - 1:
                nxt = h[(u + 1) * UNIT:(u + 1) * UNIT + HALO_ROWS]
            else:
                nxt = jnp.where(at_end, 0.0, _modulate(xn_ref[...], g, shift, scale))
            hext_ref[u, UNIT:UNIT + HALO_ROWS, :] = prev.astype(BF16)
            hext_ref[u, UNIT + HALO_ROWS:UNIT + 2 * HALO_ROWS, :] = nxt.astype(BF16)


    def zero_gaps(self, ref):
        zeros = jnp.zeros((HALO_ROWS, LANES), ref.dtype)
        for c in range(ref.shape[0]):
            ref[c, 0:HALO_ROWS, :] = zeros
            ref[c, HALO_ROWS + UNIT:2 * HALO_ROWS + UNIT, :] = zeros

    def store_padded(self, ref, r):
        for c in range(ref.shape[0]):
            cols = slice(c * LANES, (c + 1) * LANES)
            ref[c, HALO_ROWS:HALO_ROWS + UNIT, :] = r[0:UNIT, cols]
            if self.halo:
                ref[c, 0:HALO_ROWS, :] = r[UNIT:UNIT + HALO_ROWS, cols]
                ref[c, HALO_ROWS + UNIT:2 * HALO_ROWS + UNIT, :] = (
                    r[UNIT + HALO_ROWS:UNIT + 2 * HALO_ROWS, cols])

    def conv3(self, ref, w):
        b = HALO_ROWS
        out = []
        for c in range(ref.shape[0]):
            wc = w[:, c * LANES:(c + 1) * LANES]
            out.append(ref[c, b - 1:b - 1 + UNIT, :] * wc[0:1] + ref[c, b:b + UNIT, :] * wc[1:2]
                       + ref[c, b + 1:b + 1 + UNIT, :] * wc[2:3])
        return jnp.concatenate(out, axis=1)


def _ffn_kernel(*refs, units, final):
    halo, n_units = units.halo, units.n
    it = iter(refs)
    x_ref = next(it)
    xp_ref = next(it) if halo else None
    xn_ref = next(it) if halo else None
    mod_ref, g_ref, wup_ref, cw_ref, wdn_ref = (next(it) for _ in range(5))
    fg_ref = next(it) if final else None
    o_ref, hext_ref, acc_ref = (next(it) for _ in range(3))
    a_refs = [[[next(it), next(it)] for _ in range(n_units)] for _ in range(2)]

    x = x_ref[...]
    mod = mod_ref[0]
    units.fill_hext(hext_ref, x, xp_ref, xn_ref, g_ref[...], mod[3:4], mod[4:5], pl.program_id(0))
    if not halo:
        for p in range(2):
            for u in range(n_units):
                for t in range(2):
                    units.zero_gaps(a_refs[p][u][t])

    def up(u, j):
        c0 = j * FF_CHUNK
        hx = hext_ref[u]
        for t, col in enumerate((c0, D_FF + c0)):
            units.store_padded(a_refs[j % 2][u][t], _dot(hx, wup_ref[:, col:col + FF_CHUNK]))

    n_chunks = D_FF // FF_CHUNK
    for u in range(n_units):
        up(u, 0)
    for j in range(n_chunks):
        c0 = j * FF_CHUNK
        p = j % 2
        for u in range(n_units):
            if j + 1 < n_chunks:
                up(u, j + 1)
            cg = units.conv3(a_refs[p][u][0], cw_ref[:, c0:c0 + FF_CHUNK])
            cu = units.conv3(a_refs[p][u][1], cw_ref[:, D_FF + c0:D_FF + c0 + FF_CHUNK])
            act = ((cg * jax.nn.sigmoid(cg)) * cu).astype(BF16)
            part = _dot(act, wdn_ref[c0:c0 + FF_CHUNK, :])
            rows = slice(u * UNIT, (u + 1) * UNIT)
            if j == 0:
                acc_ref[rows, :] = part
            else:
                acc_ref[rows, :] += part

    y = x + mod[5:6] * acc_ref[...]
    if final:
        y = _rms(y, fg_ref[...])
    o_ref[...] = y


def _layer_resident(arr, layer):
    nd = arr.ndim - 1
    return pl.BlockSpec((None,) + arr.shape[1:], lambda *_: (layer,) + (0,) * nd,
                        pipeline_mode=pl.Buffered(1))


def _ffn(x, mod, g, wup, cw, wdn, final_g, *, layer, seq, tm):
    n = x.shape[0]
    units = _Units(n, seq, tm)
    halo = units.halo
    final = final_g is not None
    in_specs = [pl.BlockSpec((tm, D), lambda i: (i, 0))]
    args = [x]
    if halo:
        in_specs += _halo_specs(n, tm)
        args += [x, x]
    in_specs += [_mod_spec(n, mod.shape[0], tm), _layer_resident(g, layer),
                 _layer_resident(wup, layer), _layer_resident(cw, layer),
                 _layer_resident(wdn, layer)]
    args += [mod, g, wup, cw, wdn]
    if final:
        in_specs.append(_resident((1, D)))
        args.append(final_g)
    scratch = [pltpu.VMEM((units.n, units.ext_h, D), BF16), pltpu.VMEM((tm, D), F32)]
    scratch += [pltpu.VMEM((FF_CHUNK // LANES, units.ext_a, LANES), F32)] * (4 * units.n)
    return pl.pallas_call(
        functools.partial(_ffn_kernel, units=units, final=final),
        grid=(n // tm,),
        in_specs=in_specs,
        out_specs=pl.BlockSpec((tm, D), lambda i: (i, 0)),
        out_shape=jax.ShapeDtypeStruct((n, D), F32),
        scratch_shapes=scratch,
        compiler_params=_params(1),
        name="conv_ffn",
    )(*args)


class _Stage(NamedTuple):
    body: Callable
    param_specs: list
    params: list
    scratch: list


def _two_stage_body(*refs, first, second, n_first, n_second, n_first_scratch):
    x_ref, mod_ref = refs[0], refs[1]
    p1 = refs[2:2 + n_first]
    p2 = refs[2 + n_first:2 + n_first + n_second]
    o_ref = refs[2 + n_first + n_second]
    rest = refs[3 + n_first + n_second:]
    mid_ref = rest[n_first_scratch]
    first(x_ref, mod_ref, *p1, mid_ref, *rest[:n_first_scratch])
    second(mid_ref, mod_ref, *p2, o_ref, *rest[n_first_scratch + 1:])


def _run_stages(x, mod, stages, *, tm, name):
    n = x.shape[0]
    in_specs = [pl.BlockSpec((tm, D), lambda i: (i, 0)), _mod_spec(n, mod.shape[0], tm)]
    args = [x, mod]
    for st in stages:
        in_specs += st.param_specs
        args += st.params
    if len(stages) == 1:
        body, scratch = stages[0].body, stages[0].scratch
    else:
        first, second = stages
        body = functools.partial(_two_stage_body, first=first.body, second=second.body,
                                 n_first=len(first.params), n_second=len(second.params),
                                 n_first_scratch=len(first.scratch))
        scratch = first.scratch + [pltpu.VMEM((tm, D), F32)] + second.scratch
    return pl.pallas_call(
        body,
        grid=(n // tm,),
        in_specs=in_specs,
        out_specs=pl.BlockSpec((tm, D), lambda i: (i, 0)),
        out_shape=jax.ShapeDtypeStruct((n, D), F32),
        scratch_shapes=scratch,
        compiler_params=_params(1),
        name=name,
    )(*args)


def _ffn_stage(n, g, wup, cw, wdn, final_g, *, layer, seq, tm):
    units = _Units(n, seq, tm)
    assert not units.halo
    final = final_g is not None
    specs = [_layer_resident(g, layer), _layer_resident(wup, layer), _layer_resident(cw, layer),
             _layer_resident(wdn, layer)]
    params = [g, wup, cw, wdn]
    if final:
        specs.append(_resident((1, D)))
        params.append(final_g)
    scratch = [pltpu.VMEM((units.n, units.ext_h, D), BF16), pltpu.VMEM((tm, D), F32)]
    scratch += [pltpu.VMEM((FF_CHUNK // LANES, units.ext_a, LANES), F32)] * (4 * units.n)
    return _Stage(functools.partial(_ffn_kernel, units=units, final=final), specs, params, scratch)


def _sconv_kernel(*refs, units):
    halo, n_units = units.halo, units.n
    it = iter(refs)
    x_ref = next(it)
    xp_ref = next(it) if halo else None
    xn_ref = next(it) if halo else None
    mod_ref, g_ref, win_ref, cw_ref, wout_ref, o_ref, hext_ref, t_ref = (next(it) for _ in range(8))
    p_refs = [[next(it) for _ in range(n_units)] for _ in range(2)]

    x = x_ref[...]
    mod = mod_ref[0]
    units.fill_hext(hext_ref, x, xp_ref, xn_ref, g_ref[...], mod[0:1], mod[1:2], pl.program_id(0))
    if not halo:
        for p in range(2):
            for u in range(n_units):
                units.zero_gaps(p_refs[p][u])

    for j in range(D // FF_CHUNK):
        c0 = j * FF_CHUNK
        for u in range(n_units):
            hx = hext_ref[u]
            b = _dot(hx, win_ref[:, c0:c0 + FF_CHUNK])
            cg = _dot(hx, win_ref[:, D + c0:D + c0 + FF_CHUNK])
            xin = _dot(hx, win_ref[:, 2 * D + c0:2 * D + c0 + FF_CHUNK])
            units.store_padded(p_refs[j % 2][u], cg * xin)
            y = units.conv3(p_refs[j % 2][u], cw_ref[:, c0:c0 + FF_CHUNK])
            t_ref[u * UNIT:(u + 1) * UNIT, c0:c0 + FF_CHUNK] = (b[0:UNIT] * y).astype(BF16)

    o_ref[...] = x + mod[2:3] * _dot(t_ref[...], wout_ref[...])


def _sconv_stage(n, g, win, cw, wout, *, seq, tm):
    units = _Units(n, seq, tm)
    assert not units.halo
    scratch = [pltpu.VMEM((units.n, units.ext_h, D), BF16), pltpu.VMEM((tm, D), BF16)]
    scratch += [pltpu.VMEM((FF_CHUNK // LANES, units.ext_a, LANES), F32)] * (2 * units.n)
    return _Stage(functools.partial(_sconv_kernel, units=units),
                  [_resident((1, D)), _resident(win.shape), _resident(cw.shape),
                   _resident(wout.shape)],
                  [g, win, cw, wout], scratch)


def _sconv(x, mod, g, win, cw, wout, *, seq, tm):
    n = x.shape[0]
    units = _Units(n, seq, tm)
    halo = units.halo
    in_specs = [pl.BlockSpec((tm, D), lambda i: (i, 0))]
    args = [x]
    if halo:
        in_specs += _halo_specs(n, tm)
        args += [x, x]
    in_specs += [_mod_spec(n, mod.shape[0], tm), _resident((1, D)), _resident(win.shape),
                 _resident(cw.shape), _resident(wout.shape)]
    args += [mod, g, win, cw, wout]
    scratch = [pltpu.VMEM((units.n, units.ext_h, D), BF16), pltpu.VMEM((tm, D), BF16)]
    scratch += [pltpu.VMEM((FF_CHUNK // LANES, units.ext_a, LANES), F32)] * (2 * units.n)
    return pl.pallas_call(
        functools.partial(_sconv_kernel, units=units),
        grid=(n // tm,),
        in_specs=in_specs,
        out_specs=pl.BlockSpec((tm, D), lambda i: (i, 0)),
        out_shape=jax.ShapeDtypeStruct((n, D), F32),
        scratch_shapes=scratch,
        compiler_params=_params(1),
        name="short_conv",
    )(*args)


def _gelu_tanh(x):
    return 0.5 * x * (1.0 + jnp.tanh(math.sqrt(2.0 / math.pi) * (x + 0.044715 * (x * x * x))))


def _sgu_kernel(x_ref, mod_ref, g_ref, win_ref, lng_ref, ws_ref, bs_ref, wout_ref, o_ref,
                v_ref, t_ref, acc_ref, *, tm):
    x = x_ref[...]
    mod = mod_ref[0]
    h = _modulate(x, g_ref[...], mod[0:1], mod[1:2]).astype(BF16)
    v = _gelu_tanh(_dot(h, win_ref[:, D:2 * D]))
    mu = jnp.mean(v, axis=-1, keepdims=True)
    vc = v - mu
    var = jnp.mean(vc * vc, axis=-1, keepdims=True)
    v_ref[...] = ((vc * lax.rsqrt(var + EPS)) * lng_ref[...]).astype(BF16)

    n_chunks = tm // SGU_CHUNK
    n_pairs = SGU_GROUPS // 2
    pair_w = 2 * LANES

    def gate_proj(p):
        return _dot(h, win_ref[:, p * pair_w:(p + 1) * pair_w])

    u_next = gate_proj(0)
    for p in range(n_pairs):
        u_pair = _gelu_tanh(u_next)
        if p + 1 < n_pairs:
            u_next = gate_proj(p + 1)
        for gi in (2 * p, 2 * p + 1):
            cols = slice(gi * LANES, (gi + 1) * LANES)
            u = u_pair[:, (gi % 2) * LANES:(gi % 2 + 1) * LANES]
            rhs = jnp.concatenate(
                [v_ref[n * SGU_CHUNK:(n + 1) * SGU_CHUNK, cols] for n in range(n_chunks)], axis=1)
            mixed = _dot(ws_ref[gi], rhs)
            bias = bs_ref[:, cols]
            for n in range(n_chunks):
                rows = slice(n * SGU_CHUNK, (n + 1) * SGU_CHUNK)
                t_ref[rows, cols] = (
                    u[rows] * (mixed[:, n * LANES:(n + 1) * LANES] + bias)).astype(BF16)
        pcols = slice(p * pair_w, (p + 1) * pair_w)
        part = _dot(t_ref[:, pcols], wout_ref[pcols, :])
        if p == 0:
            acc_ref[...] = part
        else:
            acc_ref[...] += part

    o_ref[...] = x + mod[2:3] * acc_ref[...]


def _sgu_stage(g, win, lng, ws, bs_full, wout, *, tm):
    return _Stage(functools.partial(_sgu_kernel, tm=tm),
                  [_resident((1, D)), _resident(win.shape), _resident((1, D)), _resident(ws.shape),
                   _resident(bs_full.shape), _resident(wout.shape)],
                  [g, win, lng, ws, bs_full, wout],
                  [pltpu.VMEM((tm, D), BF16), pltpu.VMEM((tm, D), BF16), pltpu.VMEM((tm, D), F32)])


LOG2E = math.log2(math.e)
Q_SCALE = HD ** -0.5 * LOG2E


def _softmax_pv_t(parts, sink, v_t):
    sink2 = sink * LOG2E
    m = sink2
    for s_t in parts:
        m = jnp.maximum(m, jnp.max(s_t, axis=0, keepdims=True))
    den = jnp.exp2(sink2 - m)
    ps = []
    for s_t in parts:
        p = jnp.exp2(s_t - m)
        den = den + jnp.sum(p, axis=0, keepdims=True)
        ps.append(p.astype(BF16))
    p_all = ps[0] if len(ps) == 1 else jnp.concatenate(ps, axis=0)
    return _dot(v_t, p_all) / den


def _attn_prompt_kernel(x_ref, mod_ref, g_ref, wqkv_ref, wo_ref, sink_ref, o_ref, k_ref, v_ref,
                        q_s, k_s, ot_s, o_s, *, bb, seq):
    x = x_ref[...]
    mod = mod_ref[0]
    h = _modulate(x, g_ref[...], mod[0:1], mod[1:2]).astype(BF16)
    nq = N_HEADS * HD
    q_s[...] = (_dot(h, wqkv_ref[:, 0:nq]) * Q_SCALE).astype(BF16)
    k = _dot(h, wqkv_ref[:, nq:nq + KV_DIM])
    v = _dot(h, wqkv_ref[:, nq + KV_DIM:nq + 2 * KV_DIM])
    k_ref[...] = k
    v_ref[...] = v
    k_s[...] = k.astype(BF16)

    def one_batch(b, carry):
        rows = pl.ds(pl.multiple_of(b * seq, seq), seq)
        v_t = v_ref[rows, :].T.astype(BF16)
        for kv in range(N_KV):
            kh = k_s[rows, kv * HD:(kv + 1) * HD]
            vh_t = v_t[kv * HD:(kv + 1) * HD, :]
            scores = [_dot_nt(kh, q_s[rows, hd * HD:(hd + 1) * HD])
                      for hd in range(kv * Q_PER_KV, (kv + 1) * Q_PER_KV)]
            for j, s_t in enumerate(scores):
                hd = kv * Q_PER_KV + j
                ot_s[hd * HD:(hd + 1) * HD, :] = _softmax_pv_t([s_t], sink_ref[0, hd], vh_t)
        o_s[rows, :] = ot_s[...].T.astype(BF16)
        return carry

    lax.fori_loop(0, bb, one_batch, 0)
    o_ref[...] = x + mod[2:3] * _dot(o_s[...], wo_ref[...])


def _attn_prompt(x, mod, g, wqkv, wo, sink, *, seq, bb):
    n = x.shape[0]
    tm = bb * seq
    nq = N_HEADS * HD
    return pl.pallas_call(
        functools.partial(_attn_prompt_kernel, bb=bb, seq=seq),
        grid=(n // tm,),
        in_specs=[pl.BlockSpec((tm, D), lambda i: (i, 0)), _mod_spec(n, mod.shape[0], tm),
                  _resident((1, D)), _resident(wqkv.shape), _resident(wo.shape),
                  pl.BlockSpec(memory_space=pltpu.SMEM)],
        out_specs=[pl.BlockSpec((tm, D), lambda i: (i, 0)),
                   pl.BlockSpec((tm, KV_DIM), lambda i: (i, 0)),
                   pl.BlockSpec((tm, KV_DIM), lambda i: (i, 0))],
        out_shape=[jax.ShapeDtypeStruct((n, D), F32),
                   jax.ShapeDtypeStruct((n, KV_DIM), F32),
                   jax.ShapeDtypeStruct((n, KV_DIM), F32)],
        scratch_shapes=[pltpu.VMEM((tm, nq), BF16), pltpu.VMEM((tm, KV_DIM), BF16),
                        pltpu.VMEM((nq, seq), F32), pltpu.VMEM((tm, nq), BF16)],
        compiler_params=_params(1),
        name="attn_prompt",
    )(x, mod, g, wqkv, wo, sink)


def _rope_tables(seq):
    t = jnp.arange(seq)
    half = HD // 4
    inv = ROPE_THETA ** (-jnp.arange(half, dtype=F32) / half)
    ang_row = (t // GRID_W).astype(F32)[:, None] * inv[None, :]
    ang_col = (t % GRID_W).astype(F32)[:, None] * inv[None, :]
    cos_h = jnp.concatenate([jnp.cos(ang_row)] * 2 + [jnp.cos(ang_col)] * 2, axis=-1)
    sin_h = jnp.concatenate([-jnp.sin(ang_row), jnp.sin(ang_row),
                             -jnp.sin(ang_col), jnp.sin(ang_col)], axis=-1)
    reps = LANES // HD
    return jnp.tile(cos_h, (1, reps)), jnp.tile(sin_h, (1, reps))


def _qkv_rope_kernel(x_ref, mod_ref, g_ref, wqkv_ref, cos_ref, sin_ref, q_ref, k_ref, v_ref):
    x = x_ref[...]
    mod = mod_ref[0]
    h = _modulate(x, g_ref[...], mod[0:1], mod[1:2]).astype(BF16)
    nq = N_HEADS * HD
    cos = cos_ref[...]
    sin = sin_ref[...]
    lane = lax.broadcasted_iota(jnp.int32, cos.shape, 1)
    low = (lane % (HD // 2)) < (HD // 4)
    for blk in range((nq + KV_DIM) // LANES):
        c0 = blk * LANES
        if blk % 2 == 0:
            pair = _dot(h, wqkv_ref[:, c0:c0 + 2 * LANES])
        a = pair[:, (blk % 2) * LANES:(blk % 2 + 1) * LANES]
        partner = jnp.where(low, pltpu.roll(a, LANES - HD // 4, 1), pltpu.roll(a, HD // 4, 1))
        r = a * cos + partner * sin
        if c0 < nq:
            q_ref[:, c0:c0 + LANES] = (r * Q_SCALE).astype(BF16)
        else:
            k_ref[:, c0 - nq:c0 - nq + LANES] = r.astype(BF16)
    v_ref[...] = _dot(h, wqkv_ref[:, nq + KV_DIM:nq + 2 * KV_DIM]).astype(BF16)


def _qkv_rope(x, mod, g, wqkv, cos, sin, *, seq, tm):
    n = x.shape[0]
    nq = N_HEADS * HD
    tiles_per_seq = seq // tm
    return pl.pallas_call(
        _qkv_rope_kernel,
        grid=(n // tm,),
        in_specs=[pl.BlockSpec((tm, D), lambda i: (i, 0)), _mod_spec(n, mod.shape[0], tm),
                  _resident((1, D)), _resident(wqkv.shape),
                  pl.BlockSpec((tm, LANES), lambda i: (i % tiles_per_seq, 0)),
                  pl.BlockSpec((tm, LANES), lambda i: (i % tiles_per_seq, 0))],
        out_specs=[pl.BlockSpec((tm, nq), lambda i: (i, 0)),
                   pl.BlockSpec((tm, KV_DIM), lambda i: (i, 0)),
                   pl.BlockSpec((tm, KV_DIM), lambda i: (i, 0))],
        out_shape=[jax.ShapeDtypeStruct((n, nq), BF16),
                   jax.ShapeDtypeStruct((n, KV_DIM), BF16),
                   jax.ShapeDtypeStruct((n, KV_DIM), BF16)],
        compiler_params=_params(1),
        name="qkv_rope",
    )(x, mod, g, wqkv, cos, sin)


def _attn_latent_kernel(x_ref, mod_ref, q_ref, k_ref, v_ref, ck_ref, cv_ref, wo_ref, sink_ref,
                        o_ref, ot_s, *, tq, seq, span):
    qt = pl.program_id(1)
    ws = pl.multiple_of(jnp.clip(qt * tq - WINDOW, 0, seq - span), WINDOW)
    past = ck_ref.shape[1]
    kpos = ws + lax.broadcasted_iota(jnp.int32, (span, tq), 0)
    qpos = qt * tq + lax.broadcasted_iota(jnp.int32, (span, tq), 1)
    valid = jnp.abs(qpos - kpos) <= WINDOW
    v_t = jnp.concatenate([v_ref[0, pl.ds(ws, span), :].astype(F32).T,
                           cv_ref[0].astype(F32).T], axis=1).astype(BF16)

    for kv in range(N_KV):
        lanes = slice(kv * HD, (kv + 1) * HD)
        kcat = jnp.concatenate([k_ref[0, pl.ds(ws, span), lanes], ck_ref[0, :, lanes]], axis=0)
        scores = [_dot_nt(kcat, q_ref[0, :, hd * HD:(hd + 1) * HD])
                  for hd in range(kv * Q_PER_KV, (kv + 1) * Q_PER_KV)]
        for j, s_t in enumerate(scores):
            hd = kv * Q_PER_KV + j
            parts = [jnp.where(valid, s_t[0:span], NEG_BIG), s_t[span:span + past]]
            ot_s[hd * HD:(hd + 1) * HD, :] = _softmax_pv_t(parts, sink_ref[0, hd], v_t[lanes, :])

    mod = mod_ref[0]
    o_ref[0] = x_ref[0] + mod[2:3] * _dot(ot_s[...].T.astype(BF16), wo_ref[...])


def _attn_latent(x, mod, q, k, v, ck, cv, wo, sink, *, tq):
    nb, seq, _ = x.shape
    nq = N_HEADS * HD
    past = ck.shape[1]
    span = tq + 2 * WINDOW
    return pl.pallas_call(
        functools.partial(_attn_latent_kernel, tq=tq, seq=seq, span=span),
        grid=(nb, seq // tq),
        in_specs=[pl.BlockSpec((1, tq, D), lambda b, t: (b, t, 0)),
                  pl.BlockSpec((1, 6, D), lambda b, t: (b, 0, 0)),
                  pl.BlockSpec((1, tq, nq), lambda b, t: (b, t, 0)),
                  pl.BlockSpec((1, seq, KV_DIM), lambda b, t: (b, 0, 0)),
                  pl.BlockSpec((1, seq, KV_DIM), lambda b, t: (b, 0, 0)),
                  pl.BlockSpec((1, past, KV_DIM), lambda b, t: (b, 0, 0)),
                  pl.BlockSpec((1, past, KV_DIM), lambda b, t: (b, 0, 0)),
                  _resident(wo.shape),
                  pl.BlockSpec(memory_space=pltpu.SMEM)],
        out_specs=pl.BlockSpec((1, tq, D), lambda b, t: (b, t, 0)),
        out_shape=jax.ShapeDtypeStruct((nb, seq, D), F32),
        scratch_shapes=[pltpu.VMEM((nq, tq), F32)],
        compiler_params=_params(2),
        name="attn_latent",
    )(x, mod, q, k, v, ck, cv, wo, sink)


def _dft_angle(k, t, period):
    return ((k * t) % period).astype(F32) * (2.0 * math.pi / period)


def _fourier_tables(seq):
    c = jnp.arange(FOURIER_GROUP_DIM, dtype=jnp.int32)
    ang_c = _dft_angle(c[:, None], c[None, :], FOURIER_GROUP_DIM)
    chan = jnp.concatenate([jnp.cos(ang_c), jnp.sin(ang_c)], axis=1).astype(BF16)
    k = jnp.arange(seq, dtype=jnp.int32)
    hi = jnp.arange(seq // LANES, dtype=jnp.int32) * LANES
    lo = jnp.arange(LANES, dtype=jnp.int32)
    ang_hi = _dft_angle(k[:, None], hi[None, :], seq)
    ang_lo = _dft_angle(k[:, None], lo[None, :], seq)
    return chan, jnp.cos(ang_hi), jnp.sin(ang_hi), jnp.cos(ang_lo), jnp.sin(ang_lo)


def _fourier_chan_kernel(x_ref, mod_ref, g_ref, chan_ref, o_ref, *, bb, tt):
    x = x_ref[...].reshape(bb * tt, D)
    mod = mod_ref[0]
    h = _modulate(x, g_ref[...], mod[0:1], mod[1:2]).astype(BF16)
    for gi in range(D // FOURIER_GROUP_DIM):
        cols = slice(gi * LANES, (gi + 1) * LANES)
        cs = _dot(h[:, cols], chan_ref[...]).astype(BF16)
        for b in range(bb):
            rows = slice(b * tt, (b + 1) * tt)
            o_ref[b, 0, :, cols] = cs[rows, 0:LANES]
            o_ref[b, 1, :, cols] = cs[rows, LANES:2 * LANES]


def _fourier_chan(x, mod, g, chan, *, bb, tt):
    nb, seq, _ = x.shape
    n_groups = mod.shape[0]
    return pl.pallas_call(
        functools.partial(_fourier_chan_kernel, bb=bb, tt=tt),
        grid=(nb // bb, seq // tt),
        in_specs=[pl.BlockSpec((bb, tt, D), lambda b, t: (b, t, 0)),
                  pl.BlockSpec((1, 6, D), lambda b, t: ((b * bb * n_groups) // nb, 0, 0)),
                  _resident((1, D)), _resident(chan.shape)],
        out_specs=pl.BlockSpec((bb, 2, tt, D), lambda b, t: (b, 0, t, 0)),
        out_shape=jax.ShapeDtypeStruct((nb, 2, seq, D), BF16),
        compiler_params=_params(2),
        name="fourier_chan",
    )(x, mod, g, chan)


def _fourier_pos_kernel(x_ref, mod_ref, hh_ref, chi_ref, shi_ref, clo_ref, slo_ref, wout_ref,
                        o_ref, lhs_ref, *, seq, scale):
    clo = clo_ref[...]
    slo = slo_ref[...]
    n_hi = seq // LANES
    blk = min(n_hi, DFT_BLOCK // LANES)
    f = None
    for a0 in range(0, n_hi, blk):
        for a in range(a0, a0 + blk):
            ca = chi_ref[:, a:a + 1]
            sa = shi_ref[:, a:a + 1]
            lhs_ref[:, a * LANES:(a + 1) * LANES] = (ca * clo - sa * slo).astype(BF16)
            lhs_ref[:, seq + a * LANES:seq + (a + 1) * LANES] = (
                -(sa * clo + ca * slo)).astype(BF16)
        for base in (0, seq):
            cols = slice(base + a0 * LANES, base + (a0 + blk) * LANES)
            part = _dot(lhs_ref[:, cols], hh_ref[0, cols, :])
            f = part if f is None else f + part
    f = (f * scale).astype(BF16)
    mod = mod_ref[0]
    o_ref[0] = x_ref[0] + mod[2:3] * _dot(f, wout_ref[...])


def _fourier_pos(x, mod, hh, chi, shi, clo, slo, wout, *, tr):
    nb, seq, _ = x.shape
    n_groups = mod.shape[0]
    n_hi = seq // LANES
    scale = 1.0 / math.sqrt(seq * FOURIER_GROUP_DIM)
    return pl.pallas_call(
        functools.partial(_fourier_pos_kernel, seq=seq, scale=scale),
        grid=(nb, seq // tr),
        in_specs=[pl.BlockSpec((1, tr, D), lambda b, t: (b, t, 0)),
                  pl.BlockSpec((1, 6, D), lambda b, t: ((b * n_groups) // nb, 0, 0)),
                  pl.BlockSpec((1, 2 * seq, D), lambda b, t: (b, 0, 0)),
                  pl.BlockSpec((tr, n_hi), lambda b, t: (t, 0)),
                  pl.BlockSpec((tr, n_hi), lambda b, t: (t, 0)),
                  pl.BlockSpec((tr, LANES), lambda b, t: (t, 0)),
                  pl.BlockSpec((tr, LANES), lambda b, t: (t, 0)),
                  _resident(wout.shape)],
        out_specs=pl.BlockSpec((1, tr, D), lambda b, t: (b, t, 0)),
        out_shape=jax.ShapeDtypeStruct((nb, seq, D), F32),
        scratch_shapes=[pltpu.VMEM((tr, 2 * seq), BF16)],
        compiler_params=_params(2),
        name="fourier_pos",
    )(x, mod, hh, chi, shi, clo, slo, wout)


def _fourier(x, mod, g, wout, *, bb, tt, tr):
    nb, seq, _ = x.shape
    chan, chi, shi, clo, slo = _fourier_tables(seq)
    hh = _fourier_chan(x, mod, g, chan, bb=bb, tt=tt)
    return _fourier_pos(x, mod, hh.reshape(nb, 2 * seq, D), chi, shi, clo, slo, wout, tr=tr)


def _fourier_short_kernel(x_ref, mod_ref, g_ref, chan_ref, chi_ref, shi_ref, clo_ref, slo_ref,
                          wout_ref, o_ref, lhs_ref, hh_ref, f_ref, *, bb, seq, scale):
    @pl.when(pl.program_id(0) == 0)
    def _():
        clo = clo_ref[...]
        slo = slo_ref[...]
        for a in range(seq // LANES):
            ca = chi_ref[:, a:a + 1]
            sa = shi_ref[:, a:a + 1]
            lhs_ref[:, a * LANES:(a + 1) * LANES] = (ca * clo - sa * slo).astype(BF16)
            lhs_ref[:, seq + a * LANES:seq + (a + 1) * LANES] = (-(sa * clo + ca * slo)).astype(BF16)

    x = x_ref[...]
    mod = mod_ref[0]
    h = _modulate(x, g_ref[...], mod[0:1], mod[1:2]).astype(BF16)
    for gi in range(D // FOURIER_GROUP_DIM):
        cols = slice(gi * LANES, (gi + 1) * LANES)
        cs = _dot(h[:, cols], chan_ref[...]).astype(BF16)
        for b in range(bb):
            rows = slice(b * seq, (b + 1) * seq)
            hh_ref[b, 0:seq, cols] = cs[rows, 0:LANES]
            hh_ref[b, seq:2 * seq, cols] = cs[rows, LANES:2 * LANES]
    for b in range(bb):
        f_ref[b * seq:(b + 1) * seq, :] = (_dot(lhs_ref[...], hh_ref[b]) * scale).astype(BF16)
    o_ref[...] = x + mod[2:3] * _dot(f_ref[...], wout_ref[...])


def _fourier_short_stage(g, wout, *, seq, tm):
    bb = tm // seq
    chan, chi, shi, clo, slo = _fourier_tables(seq)
    scale = 1.0 / math.sqrt(seq * FOURIER_GROUP_DIM)
    params = [g, chan, chi, shi, clo, slo, wout]
    return _Stage(functools.partial(_fourier_short_kernel, bb=bb, seq=seq, scale=scale),
                  [_resident(p.shape) for p in params], params,
                  [pltpu.VMEM((seq, 2 * seq), BF16), pltpu.VMEM((bb, 2 * seq, D), BF16),
                   pltpu.VMEM((tm, D), BF16)])


def kernel(x_prompt, x_sample, cache_k, cache_v, c, c_ctx, ada_w, ada_b, norm_mix_g, norm_ffn_g,
           final_g, attn_wqkv, attn_wo, attn_sink, sgu_w_in, sgu_ln_g, sgu_w_s, sgu_b_s,
           sgu_w_out, sc_w_in, sc_conv, sc_w_out, fn_w_out, ffn_w_up, ffn_conv, ffn_w_down):
    nbp, seq_p, _ = x_prompt.shape
    nbs, seq_s, _ = x_sample.shape
    depth = ada_w.shape[0]
    tm = TOKEN_TILE

    cvec = jnp.concatenate([c_ctx[None, :], c, jnp.zeros((SUBLANES - 1 - nbs, D), F32)], axis=0)
    mods = _ada_mod(cvec, ada_w, ada_b).reshape(depth, SUBLANES, 6, D)

    xp = x_prompt.reshape(nbp * seq_p, D)
    xs = x_sample.reshape(nbs * seq_s, D)
    new_k = new_v = None
    wup = ffn_w_up.astype(BF16)
    wdn = ffn_w_down.astype(BF16)
    g_ffn = norm_ffn_g[:, None, :]

    for l in range(depth):
        kind, j = l % 4, l // 4
        mp = mods[l, 0:1]
        ms = mods[l, 1:1 + nbs]
        g_mix = norm_mix_g[l][None, :]
        fg = final_g[None, :] if l == depth - 1 else None
        ffn_p = _ffn_stage(xp.shape[0], g_ffn, wup, ffn_conv, wdn, fg, layer=l, seq=seq_p, tm=tm)
        if kind == 0:
            wqkv = attn_wqkv[j].astype(BF16)
            wo = attn_wo[j].astype(BF16)
            sink = attn_sink[j][None, :]
            xp, new_k, new_v = _attn_prompt(xp, mp, g_mix, wqkv, wo, sink, seq=seq_p,
                                            bb=PROMPT_SEQS_PER_STEP)
            xp = _run_stages(xp, mp, [ffn_p], tm=tm, name="conv_ffn")
            cos, sin = _rope_tables(seq_s)
            q, k, v = _qkv_rope(xs, ms, g_mix, wqkv, cos, sin, seq=seq_s, tm=tm)
            past = cache_k.shape[2]
            ck = cache_k[:, j].reshape(nbs, past, KV_DIM).astype(BF16)
            cv = cache_v[:, j].reshape(nbs, past, KV_DIM).astype(BF16)
            xs = _attn_latent(xs.reshape(nbs, seq_s, D), ms, q.reshape(nbs, seq_s, N_HEADS * HD),
                              k.reshape(nbs, seq_s, KV_DIM), v.reshape(nbs, seq_s, KV_DIM),
                              ck, cv, wo, sink, tq=QUERY_TILE).reshape(nbs * seq_s, D)
        elif kind == 1:
            win = sgu_w_in[j].astype(BF16)
            ws = sgu_w_s[j].astype(BF16)
            wout = sgu_w_out[j].astype(BF16)
            lng = sgu_ln_g[j][None, :]
            bs_full = jnp.repeat(sgu_b_s[j].T, LANES, axis=1)
            sgu = _sgu_stage(g_mix, win, lng, ws, bs_full, wout, tm=tm)
            xp = _run_stages(xp, mp, [sgu, ffn_p], tm=tm, name="sgu_ffn")
            xs = _run_stages(xs, ms, [sgu], tm=tm, name="sgu")
        elif kind == 2:
            win = sc_w_in[j].astype(BF16)
            wout = sc_w_out[j].astype(BF16)
            sconv_p = _sconv_stage(xp.shape[0], g_mix, win, sc_conv[j], wout, seq=seq_p, tm=tm)
            xp = _run_stages(xp, mp, [sconv_p, ffn_p], tm=tm, name="short_conv_ffn")
            xs = _sconv(xs, ms, g_mix, win, sc_conv[j], wout, seq=seq_s, tm=tm)
        else:
            wout = fn_w_out[j].astype(BF16)
            fnet_p = _fourier_short_stage(g_mix, wout, seq=seq_p, tm=tm)
            xp = _run_stages(xp, mp, [fnet_p, ffn_p], tm=tm, name="fourier_ffn")
            xs = _fourier(xs.reshape(nbs, seq_s, D), ms, g_mix, wout,
                          bb=1, tt=tm, tr=tm).reshape(nbs * seq_s, D)

        xs = _ffn(xs, ms, g_ffn, wup, ffn_conv, wdn, fg, layer=l, seq=seq_s, tm=tm)

    y_prompt = xp.reshape(nbp, seq_p, D)
    y_sample = xs.reshape(nbs, seq_s, D)
    new_cache_k = new_k.reshape(nbp, 1, seq_p, N_KV, HD)
    new_cache_v = new_v.reshape(nbp, 1, seq_p, N_KV, HD)
    return (y_prompt, y_sample, new_cache_k, new_cache_v)
```

```python
import functools
import math
from typing import Callable, NamedTuple

import jax
import jax.numpy as jnp
from jax import lax
from jax.experimental import pallas as pl
from jax.experimental.pallas import tpu as pltpu

D = 1024
N_HEADS = 16
N_KV = 4
HD = 64
Q_PER_KV = N_HEADS // N_KV
KV_DIM = N_KV * HD
WINDOW = 128
GRID_W = 64
ROPE_THETA = 10000.0
SGU_CHUNK = 128
SGU_GROUPS = 8
FOURIER_GROUP_DIM = 128
D_FF = 2816
EPS = 1e-6
NEG_BIG = -1e30

LANES = 128
SUBLANES = 8
HALO_ROWS = SUBLANES
TOKEN_TILE = 512
QUERY_TILE = 256
PROMPT_SEQS_PER_STEP = 4
FF_CHUNK = 256
UNIT = 256
DFT_BLOCK = 512
VMEM_LIMIT = 52 * 1024 * 1024

F32 = jnp.float32
BF16 = jnp.bfloat16


def _resident(shape):
    nd = len(shape)
    return pl.BlockSpec(shape, lambda *_: (0,) * nd, pipeline_mode=pl.Buffered(1))


def _params(n_axes, flags=None):
    return pltpu.CompilerParams(dimension_semantics=("arbitrary",) * n_axes,
                                vmem_limit_bytes=VMEM_LIMIT, flags=flags)


def _modulate(x, g, shift, scale):
    ms = jnp.mean(x * x, axis=-1, keepdims=True)
    return (x * lax.rsqrt(ms + EPS)) * (g * (1.0 + scale)) + shift


def _rms(x, g):
    ms = jnp.mean(x * x, axis=-1, keepdims=True)
    return (x * lax.rsqrt(ms + EPS)) * g


def _dot(a, b):
    return jnp.dot(a, b, preferred_element_type=F32)


def _dot_nt(a, b):
    return lax.dot_general(a, b, (((1,), (1,)), ((), ())), preferred_element_type=F32)


def _ada_kernel(c_ref, w_ref, b_ref, o_ref):
    c = c_ref[...]
    s = (c * jax.nn.sigmoid(c)).astype(BF16)
    o_ref[0] = _dot(s, w_ref[0].astype(BF16)) + b_ref[0]


def _ada_mod(cvec, ada_w, ada_b):
    depth, _, n6 = ada_w.shape
    bn = 2 * D
    return pl.pallas_call(
        _ada_kernel,
        grid=(depth, n6 // bn),
        in_specs=[pl.BlockSpec((SUBLANES, D), lambda l, j: (0, 0)),
                  pl.BlockSpec((1, D, bn), lambda l, j: (l, 0, j)),
                  pl.BlockSpec((1, 1, bn), lambda l, j: (l, 0, j))],
        out_specs=pl.BlockSpec((1, SUBLANES, bn), lambda l, j: (l, 0, j)),
        out_shape=jax.ShapeDtypeStruct((depth, SUBLANES, n6), F32),
        compiler_params=_params(2),
        name="ada_mod",
    )(cvec, ada_w, ada_b.reshape(depth, 1, n6))


def _halo_specs(n_rows, tm):
    t8 = tm // HALO_ROWS
    last = n_rows // HALO_ROWS - 1
    return [pl.BlockSpec((HALO_ROWS, D), lambda i: (jnp.maximum(i * t8 - 1, 0), 0)),
            pl.BlockSpec((HALO_ROWS, D), lambda i: (jnp.minimum((i + 1) * t8, last), 0))]


def _mod_spec(n_rows, n_groups, tm):
    rows_per_group = n_rows // n_groups
    return pl.BlockSpec((1, 6, D), lambda i: ((i * tm) // rows_per_group, 0, 0))


class _Units:
    def __init__(self, n_rows, seq, tm):
        assert n_rows % tm == 0 and tm % UNIT == 0
        self.tm, self.seq = tm, seq
        self.n = tm // UNIT
        self.halo = seq > UNIT
        assert seq == UNIT or (seq > tm and seq % tm == 0)
        self.ext_h = UNIT + 2 * HALO_ROWS if self.halo else UNIT
        self.ext_a = UNIT + 2 * HALO_ROWS

    def fill_hext(self, hext_ref, x, xp_ref, xn_ref, g, shift, scale, tile_idx):
        h = _modulate(x, g, shift, scale)
        for u in range(self.n):
            hext_ref[u, 0:UNIT, :] = h[u * UNIT:(u + 1) * UNIT].astype(BF16)
        if not self.halo:
            return
        tiles_per_seq = self.seq // self.tm
        at_start = (tile_idx % tiles_per_seq) == 0
        at_end = (tile_idx % tiles_per_seq) == tiles_per_seq - 1
        for u in range(self.n):
            if u > 0:
                prev = h[u * UNIT - HALO_ROWS:u * UNIT]
            else:
                prev = jnp.where(at_start, 0.0, _modulate(xp_ref[...], g, shift, scale))
            if u < self.n - 1:
                nxt = h[(u + 1) * UNIT:(u + 1) * UNIT + HALO_ROWS]
            else:
                nxt = jnp.where(at_end, 0.0, _modulate(xn_ref[...], g, shift, scale))
            hext_ref[u, UNIT:UNIT + HALO_ROWS, :] = prev.astype(BF16)
            hext_ref[u, UNIT + HALO_ROWS:UNIT + 2 * HALO_ROWS, :] = nxt.astype(BF16)


    def zero_gaps(self, ref):
        zeros = jnp.zeros((HALO_ROWS, LANES), ref.dtype)
        for c in range(ref.shape[0]):
            ref[c, 0:HALO_ROWS, :] = zeros
            ref[c, HALO_ROWS + UNIT:2 * HALO_ROWS + UNIT, :] = zeros

    def store_padded(self, ref, r):
        for c in range(ref.shape[0]):
            cols = slice(c * LANES, (c + 1) * LANES)
            ref[c, HALO_ROWS:HALO_ROWS + UNIT, :] = r[0:UNIT, cols]
            if self.halo:
                ref[c, 0:HALO_ROWS, :] = r[UNIT:UNIT + HALO_ROWS, cols]
                ref[c, HALO_ROWS + UNIT:2 * HALO_ROWS + UNIT, :] = (
                    r[UNIT + HALO_ROWS:UNIT + 2 * HALO_ROWS, cols])

    def conv3(self, ref, w):
        b = HALO_ROWS
        out = []
        for c in range(ref.shape[0]):
            wc = w[:, c * LANES:(c + 1) * LANES]
            out.append(ref[c, b - 1:b - 1 + UNIT, :] * wc[0:1] + ref[c, b:b + UNIT, :] * wc[1:2]
                       + ref[c, b + 1:b + 1 + UNIT, :] * wc[2:3])
        return jnp.concatenate(out, axis=1)


def _ffn_kernel(*refs, units, final):
    halo, n_units = units.halo, units.n
    it = iter(refs)
    x_ref = next(it)
    xp_ref = next(it) if halo else None
    xn_ref = next(it) if halo else None
    mod_ref, g_ref, wup_ref, cw_ref, wdn_ref = (next(it) for _ in range(5))
    fg_ref = next(it) if final else None
    o_ref, hext_ref, acc_ref = (next(it) for _ in range(3))
    a_refs = [[[next(it), next(it)] for _ in range(n_units)] for _ in range(2)]

    x = x_ref[...]
    mod = mod_ref[0]
    units.fill_hext(hext_ref, x, xp_ref, xn_ref, g_ref[...], mod[3:4], mod[4:5], pl.program_id(0))
    if not halo:
        for p in range(2):
            for u in range(n_units):
                for t in range(2):
                    units.zero_gaps(a_refs[p][u][t])

    def up(u, j):
        c0 = j * FF_CHUNK
        hx = hext_ref[u]
        for t, col in enumerate((c0, D_FF + c0)):
            units.store_padded(a_refs[j % 2][u][t], _dot(hx, wup_ref[:, col:col + FF_CHUNK]))

    n_chunks = D_FF // FF_CHUNK
    for u in range(n_units):
        up(u, 0)
    for j in range(n_chunks):
        c0 = j * FF_CHUNK
        p = j % 2
        for u in range(n_units):
            if j + 1 < n_chunks:
                up(u, j + 1)
            cg = units.conv3(a_refs[p][u][0], cw_ref[:, c0:c0 + FF_CHUNK])
            cu = units.conv3(a_refs[p][u][1], cw_ref[:, D_FF + c0:D_FF + c0 + FF_CHUNK])
            act = ((cg * jax.nn.sigmoid(cg)) * cu).astype(BF16)
            part = _dot(act, wdn_ref[c0:c0 + FF_CHUNK, :])
            rows = slice(u * UNIT, (u + 1) * UNIT)
            if j == 0:
                acc_ref[rows, :] = part
            else:
                acc_ref[rows, :] += part

    y = x + mod[5:6] * acc_ref[...]
    if final:
        y = _rms(y, fg_ref[...])
    o_ref[...] = y


def _layer_resident(arr, layer):
    nd = arr.ndim - 1
    return pl.BlockSpec((None,) + arr.shape[1:], lambda *_: (layer,) + (0,) * nd,
                        pipeline_mode=pl.Buffered(1))


def _ffn(x, mod, g, wup, cw, wdn, final_g, *, layer, seq, tm):
    n = x.shape[0]
    units = _Units(n, seq, tm)
    halo = units.halo
    final = final_g is not None
    in_specs = [pl.BlockSpec((tm, D), lambda i: (i, 0))]
    args = [x]
    if halo:
        in_specs += _halo_specs(n, tm)
        args += [x, x]
    in_specs += [_mod_spec(n, mod.shape[0], tm), _layer_resident(g, layer),
                 _layer_resident(wup, layer), _layer_resident(cw, layer),
                 _layer_resident(wdn, layer)]
    args += [mod, g, wup, cw, wdn]
    if final:
        in_specs.append(_resident((1, D)))
        args.append(final_g)
    scratch = [pltpu.VMEM((units.n, units.ext_h, D), BF16), pltpu.VMEM((tm, D), F32)]
    scratch += [pltpu.VMEM((FF_CHUNK // LANES, units.ext_a, LANES), F32)] * (4 * units.n)
    return pl.pallas_call(
        functools.partial(_ffn_kernel, units=units, final=final),
        grid=(n // tm,),
        in_specs=in_specs,
        out_specs=pl.BlockSpec((tm, D), lambda i: (i, 0)),
        out_shape=jax.ShapeDtypeStruct((n, D), F32),
        scratch_shapes=scratch,
        compiler_params=_params(1),
        name="conv_ffn",
    )(*args)


class _Stage(NamedTuple):
    body: Callable
    param_specs: list
    params: list
    scratch: list


def _two_stage_body(*refs, first, second, n_first, n_second, n_first_scratch):
    x_ref, mod_ref = refs[0], refs[1]
    p1 = refs[2:2 + n_first]
    p2 = refs[2 + n_first:2 + n_first + n_second]
    o_ref = refs[2 + n_first + n_second]
    rest = refs[3 + n_first + n_second:]
    mid_ref = rest[n_first_scratch]
    first(x_ref, mod_ref, *p1, mid_ref, *rest[:n_first_scratch])
    second(mid_ref, mod_ref, *p2, o_ref, *rest[n_first_scratch + 1:])


def _run_stages(x, mod, stages, *, tm, name):
    n = x.shape[0]
    in_specs = [pl.BlockSpec((tm, D), lambda i: (i, 0)), _mod_spec(n, mod.shape[0], tm)]
    args = [x, mod]
    for st in stages:
        in_specs += st.param_specs
        args += st.params
    if len(stages) == 1:
        body, scratch = stages[0].body, stages[0].scratch
    else:
        first, second = stages
        body = functools.partial(_two_stage_body, first=first.body, second=second.body,
                                 n_first=len(first.params), n_second=len(second.params),
                                 n_first_scratch=len(first.scratch))
        scratch = first.scratch + [pltpu.VMEM((tm, D), F32)] + second.scratch
    return pl.pallas_call(
        body,
        grid=(n // tm,),
        in_specs=in_specs,
        out_specs=pl.BlockSpec((tm, D), lambda i: (i, 0)),
        out_shape=jax.ShapeDtypeStruct((n, D), F32),
        scratch_shapes=scratch,
        compiler_params=_params(1),
        name=name,
    )(*args)


def _ffn_stage(n, g, wup, cw, wdn, final_g, *, layer, seq, tm):
    units = _Units(n, seq, tm)
    assert not units.halo
    final = final_g is not None
    specs = [_layer_resident(g, layer), _layer_resident(wup, layer), _layer_resident(cw, layer),
             _layer_resident(wdn, layer)]
    params = [g, wup, cw, wdn]
    if final:
        specs.append(_resident((1, D)))
        params.append(final_g)
    scratch = [pltpu.VMEM((units.n, units.ext_h, D), BF16), pltpu.VMEM((tm, D), F32)]
    scratch += [pltpu.VMEM((FF_CHUNK // LANES, units.ext_a, LANES), F32)] * (4 * units.n)
    return _Stage(functools.partial(_ffn_kernel, units=units, final=final), specs, params, scratch)


def _sconv_kernel(*refs, units):
    halo, n_units = units.halo, units.n
    it = iter(refs)
    x_ref = next(it)
    xp_ref = next(it) if halo else None
    xn_ref = next(it) if halo else None
    mod_ref, g_ref, win_ref, cw_ref, wout_ref, o_ref, hext_ref, t_ref = (next(it) for _ in range(8))
    p_refs = [[next(it) for _ in range(n_units)] for _ in range(2)]

    x = x_ref[...]
    mod = mod_ref[0]
    units.fill_hext(hext_ref, x, xp_ref, xn_ref, g_ref[...], mod[0:1], mod[1:2], pl.program_id(0))
    if not halo:
        for p in range(2):
            for u in range(n_units):
                units.zero_gaps(p_refs[p][u])

    for j in range(D // FF_CHUNK):
        c0 = j * FF_CHUNK
        for u in range(n_units):
            hx = hext_ref[u]
            b = _dot(hx, win_ref[:, c0:c0 + FF_CHUNK])
            cg = _dot(hx, win_ref[:, D + c0:D + c0 + FF_CHUNK])
            xin = _dot(hx, win_ref[:, 2 * D + c0:2 * D + c0 + FF_CHUNK])
            units.store_padded(p_refs[j % 2][u], cg * xin)
            y = units.conv3(p_refs[j % 2][u], cw_ref[:, c0:c0 + FF_CHUNK])
            t_ref[u * UNIT:(u + 1) * UNIT, c0:c0 + FF_CHUNK] = (b[0:UNIT] * y).astype(BF16)

    o_ref[...] = x + mod[2:3] * _dot(t_ref[...], wout_ref[...])


def _sconv_stage(n, g, win, cw, wout, *, seq, tm):
    units = _Units(n, seq, tm)
    assert not units.halo
    scratch = [pltpu.VMEM((units.n, units.ext_h, D), BF16), pltpu.VMEM((tm, D), BF16)]
    scratch += [pltpu.VMEM((FF_CHUNK // LANES, units.ext_a, LANES), F32)] * (2 * units.n)
    return _Stage(functools.partial(_sconv_kernel, units=units),
                  [_resident((1, D)), _resident(win.shape), _resident(cw.shape),
                   _resident(wout.shape)],
                  [g, win, cw, wout], scratch)


def _sconv(x, mod, g, win, cw, wout, *, seq, tm):
    n = x.shape[0]
    units = _Units(n, seq, tm)
    halo = units.halo
    in_specs = [pl.BlockSpec((tm, D), lambda i: (i, 0))]
    args = [x]
    if halo:
        in_specs += _halo_specs(n, tm)
        args += [x, x]
    in_specs += [_mod_spec(n, mod.shape[0], tm), _resident((1, D)), _resident(win.shape),
                 _resident(cw.shape), _resident(wout.shape)]
    args += [mod, g, win, cw, wout]
    scratch = [pltpu.VMEM((units.n, units.ext_h, D), BF16), pltpu.VMEM((tm, D), BF16)]
    scratch += [pltpu.VMEM((FF_CHUNK // LANES, units.ext_a, LANES), F32)] * (2 * units.n)
    return pl.pallas_call(
        functools.partial(_sconv_kernel, units=units),
        grid=(n // tm,),
        in_specs=in_specs,
        out_specs=pl.BlockSpec((tm, D), lambda i: (i, 0)),
        out_shape=jax.ShapeDtypeStruct((n, D), F32),
        scratch_shapes=scratch,
        compiler_params=_params(1),
        name="short_conv",
    )(*args)


def _gelu_tanh(x):
    return 0.5 * x * (1.0 + jnp.tanh(math.sqrt(2.0 / math.pi) * (x + 0.044715 * (x * x * x))))


def _sgu_kernel(x_ref, mod_ref, g_ref, win_ref, lng_ref, ws_ref, bs_ref, wout_ref, o_ref,
                v_ref, t_ref, acc_ref, *, tm):
    x = x_ref[...]
    mod = mod_ref[0]
    h = _modulate(x, g_ref[...], mod[0:1], mod[1:2]).astype(BF16)
    v = _gelu_tanh(_dot(h, win_ref[:, D:2 * D]))
    mu = jnp.mean(v, axis=-1, keepdims=True)
    vc = v - mu
    var = jnp.mean(vc * vc, axis=-1, keepdims=True)
    v_ref[...] = ((vc * lax.rsqrt(var + EPS)) * lng_ref[...]).astype(BF16)

    n_chunks = tm // SGU_CHUNK
    n_pairs = SGU_GROUPS // 2
    pair_w = 2 * LANES

    def gate_proj(p):
        return _dot(h, win_ref[:, p * pair_w:(p + 1) * pair_w])

    u_next = gate_proj(0)
    for p in range(n_pairs):
        u_pair = _gelu_tanh(u_next)
        if p + 1 < n_pairs:
            u_next = gate_proj(p + 1)
        for gi in (2 * p, 2 * p + 1):
            cols = slice(gi * LANES, (gi + 1) * LANES)
            u = u_pair[:, (gi % 2) * LANES:(gi % 2 + 1) * LANES]
            rhs = jnp.concatenate(
                [v_ref[n * SGU_CHUNK:(n + 1) * SGU_CHUNK, cols] for n in range(n_chunks)], axis=1)
            mixed = _dot(ws_ref[gi], rhs)
            bias = bs_ref[:, cols]
            for n in range(n_chunks):
                rows = slice(n * SGU_CHUNK, (n + 1) * SGU_CHUNK)
                t_ref[rows, cols] = (
                    u[rows] * (mixed[:, n * LANES:(n + 1) * LANES] + bias)).astype(BF16)
        pcols = slice(p * pair_w, (p + 1) * pair_w)
        part = _dot(t_ref[:, pcols], wout_ref[pcols, :])
        if p == 0:
            acc_ref[...] = part
        else:
            acc_ref[...] += part

    o_ref[...] = x + mod[2:3] * acc_ref[...]


def _sgu_stage(g, win, lng, ws, bs_full, wout, *, tm):
    return _Stage(functools.partial(_sgu_kernel, tm=tm),
                  [_resident((1, D)), _resident(win.shape), _resident((1, D)), _resident(ws.shape),
                   _resident(bs_full.shape), _resident(wout.shape)],
                  [g, win, lng, ws, bs_full, wout],
                  [pltpu.VMEM((tm, D), BF16), pltpu.VMEM((tm, D), BF16), pltpu.VMEM((tm, D), F32)])


LOG2E = math.log2(math.e)
Q_SCALE = HD ** -0.5 * LOG2E


def _softmax_pv_t(parts, sink, v_t):
    sink2 = sink * LOG2E
    m = sink2
    for s_t in parts:
        m = jnp.maximum(m, jnp.max(s_t, axis=0, keepdims=True))
    den = jnp.exp2(sink2 - m)
    ps = []
    for s_t in parts:
        p = jnp.exp2(s_t - m)
        den = den + jnp.sum(p, axis=0, keepdims=True)
        ps.append(p.astype(BF16))
    p_all = ps[0] if len(ps) == 1 else jnp.concatenate(ps, axis=0)
    return _dot(v_t, p_all) / den


def _attn_prompt_kernel(x_ref, mod_ref, g_ref, wqkv_ref, wo_ref, sink_ref, o_ref, k_ref, v_ref,
                        q_s, k_s, ot_s, o_s, *, bb, seq):
    x = x_ref[...]
    mod = mod_ref[0]
    h = _modulate(x, g_ref[...], mod[0:1], mod[1:2]).astype(BF16)
    nq = N_HEADS * HD
    q_s[...] = (_dot(h, wqkv_ref[:, 0:nq]) * Q_SCALE).astype(BF16)
    k = _dot(h, wqkv_ref[:, nq:nq + KV_DIM])
    v = _dot(h, wqkv_ref[:, nq + KV_DIM:nq + 2 * KV_DIM])
    k_ref[...] = k
    v_ref[...] = v
    k_s[...] = k.astype(BF16)

    def one_batch(b, carry):
        rows = pl.ds(pl.multiple_of(b * seq, seq), seq)
        v_t = v_ref[rows, :].T.astype(BF16)
        for kv in range(N_KV):
            kh = k_s[rows, kv * HD:(kv + 1) * HD]
            vh_t = v_t[kv * HD:(kv + 1) * HD, :]
            scores = [_dot_nt(kh, q_s[rows, hd * HD:(hd + 1) * HD])
                      for hd in range(kv * Q_PER_KV, (kv + 1) * Q_PER_KV)]
            for j, s_t in enumerate(scores):
                hd = kv * Q_PER_KV + j
                ot_s[hd * HD:(hd + 1) * HD, :] = _softmax_pv_t([s_t], sink_ref[0, hd], vh_t)
        o_s[rows, :] = ot_s[...].T.astype(BF16)
        return carry

    lax.fori_loop(0, bb, one_batch, 0, unroll=2)
    o_ref[...] = x + mod[2:3] * _dot(o_s[...], wo_ref[...])


def _attn_prompt(x, mod, g, wqkv, wo, sink, *, seq, bb):
    n = x.shape[0]
    tm = bb * seq
    nq = N_HEADS * HD
    return pl.pallas_call(
        functools.partial(_attn_prompt_kernel, bb=bb, seq=seq),
        grid=(n // tm,),
        in_specs=[pl.BlockSpec((tm, D), lambda i: (i, 0)), _mod_spec(n, mod.shape[0], tm),
                  _resident((1, D)), _resident(wqkv.shape), _resident(wo.shape),
                  pl.BlockSpec(memory_space=pltpu.SMEM)],
        out_specs=[pl.BlockSpec((tm, D), lambda i: (i, 0)),
                   pl.BlockSpec((tm, KV_DIM), lambda i: (i, 0)),
                   pl.BlockSpec((tm, KV_DIM), lambda i: (i, 0))],
        out_shape=[jax.ShapeDtypeStruct((n, D), F32),
                   jax.ShapeDtypeStruct((n, KV_DIM), F32),
                   jax.ShapeDtypeStruct((n, KV_DIM), F32)],
        scratch_shapes=[pltpu.VMEM((tm, nq), BF16), pltpu.VMEM((tm, KV_DIM), BF16),
                        pltpu.VMEM((nq, seq), F32), pltpu.VMEM((tm, nq), BF16)],
        compiler_params=_params(1),
        name="attn_prompt",
    )(x, mod, g, wqkv, wo, sink)


def _rope_tables(seq):
    t = jnp.arange(seq)
    half = HD // 4
    inv = ROPE_THETA ** (-jnp.arange(half, dtype=F32) / half)
    ang_row = (t // GRID_W).astype(F32)[:, None] * inv[None, :]
    ang_col = (t % GRID_W).astype(F32)[:, None] * inv[None, :]
    cos_h = jnp.concatenate([jnp.cos(ang_row)] * 2 + [jnp.cos(ang_col)] * 2, axis=-1)
    sin_h = jnp.concatenate([-jnp.sin(ang_row), jnp.sin(ang_row),
                             -jnp.sin(ang_col), jnp.sin(ang_col)], axis=-1)
    reps = LANES // HD
    return jnp.tile(cos_h, (1, reps)), jnp.tile(sin_h, (1, reps))


def _qkv_rope_kernel(x_ref, mod_ref, g_ref, wqkv_ref, cos_ref, sin_ref, q_ref, k_ref, v_ref):
    x = x_ref[...]
    mod = mod_ref[0]
    h = _modulate(x, g_ref[...], mod[0:1], mod[1:2]).astype(BF16)
    nq = N_HEADS * HD
    cos = cos_ref[...]
    sin = sin_ref[...]
    lane = lax.broadcasted_iota(jnp.int32, cos.shape, 1)
    low = (lane % (HD // 2)) < (HD // 4)
    for blk in range((nq + KV_DIM) // LANES):
        c0 = blk * LANES
        if blk % 2 == 0:
            pair = _dot(h, wqkv_ref[:, c0:c0 + 2 * LANES])
        a = pair[:, (blk % 2) * LANES:(blk % 2 + 1) * LANES]
        partner = jnp.where(low, pltpu.roll(a, LANES - HD // 4, 1), pltpu.roll(a, HD // 4, 1))
        r = a * cos + partner * sin
        if c0 < nq:
            q_ref[:, c0:c0 + LANES] = (r * Q_SCALE).astype(BF16)
        else:
            k_ref[:, c0 - nq:c0 - nq + LANES] = r.astype(BF16)
    v_ref[...] = _dot(h, wqkv_ref[:, nq + KV_DIM:nq + 2 * KV_DIM]).astype(BF16)


def _qkv_rope(x, mod, g, wqkv, cos, sin, *, seq, tm):
    n = x.shape[0]
    nq = N_HEADS * HD
    tiles_per_seq = seq // tm
    return pl.pallas_call(
        _qkv_rope_kernel,
        grid=(n // tm,),
        in_specs=[pl.BlockSpec((tm, D), lambda i: (i, 0)), _mod_spec(n, mod.shape[0], tm),
                  _resident((1, D)), _resident(wqkv.shape),
                  pl.BlockSpec((tm, LANES), lambda i: (i % tiles_per_seq, 0)),
                  pl.BlockSpec((tm, LANES), lambda i: (i % tiles_per_seq, 0))],
        out_specs=[pl.BlockSpec((tm, nq), lambda i: (i, 0)),
                   pl.BlockSpec((tm, KV_DIM), lambda i: (i, 0)),
                   pl.BlockSpec((tm, KV_DIM), lambda i: (i, 0))],
        out_shape=[jax.ShapeDtypeStruct((n, nq), BF16),
                   jax.ShapeDtypeStruct((n, KV_DIM), BF16),
                   jax.ShapeDtypeStruct((n, KV_DIM), BF16)],
        compiler_params=_params(1),
        name="qkv_rope",
    )(x, mod, g, wqkv, cos, sin)


def _attn_latent_kernel(x_ref, mod_ref, q_ref, k_ref, v_ref, ck_ref, cv_ref, wo_ref, sink_ref,
                        o_ref, ot_s, *, tq, seq, span):
    qt = pl.program_id(1)
    ws = pl.multiple_of(jnp.clip(qt * tq - WINDOW, 0, seq - span), WINDOW)
    past = ck_ref.shape[1]
    kpos = ws + lax.broadcasted_iota(jnp.int32, (span, tq), 0)
    qpos = qt * tq + lax.broadcasted_iota(jnp.int32, (span, tq), 1)
    valid = jnp.abs(qpos - kpos) <= WINDOW
    v_t = jnp.concatenate([v_ref[0, pl.ds(ws, span), :].astype(F32).T,
                           cv_ref[0].astype(F32).T], axis=1).astype(BF16)

    for kv in range(N_KV):
        lanes = slice(kv * HD, (kv + 1) * HD)
        kcat = jnp.concatenate([k_ref[0, pl.ds(ws, span), lanes], ck_ref[0, :, lanes]], axis=0)
        scores = [_dot_nt(kcat, q_ref[0, :, hd * HD:(hd + 1) * HD])
                  for hd in range(kv * Q_PER_KV, (kv + 1) * Q_PER_KV)]
        for j, s_t in enumerate(scores):
            hd = kv * Q_PER_KV + j
            parts = [jnp.where(valid, s_t[0:span], NEG_BIG), s_t[span:span + past]]
            ot_s[hd * HD:(hd + 1) * HD, :] = _softmax_pv_t(parts, sink_ref[0, hd], v_t[lanes, :])

    mod = mod_ref[0]
    o_ref[0] = x_ref[0] + mod[2:3] * _dot(ot_s[...].T.astype(BF16), wo_ref[...])


def _attn_latent(x, mod, q, k, v, ck, cv, wo, sink, *, tq):
    nb, seq, _ = x.shape
    nq = N_HEADS * HD
    past = ck.shape[1]
    span = tq + 2 * WINDOW
    return pl.pallas_call(
        functools.partial(_attn_latent_kernel, tq=tq, seq=seq, span=span),
        grid=(nb, seq // tq),
        in_specs=[pl.BlockSpec((1, tq, D), lambda b, t: (b, t, 0)),
                  pl.BlockSpec((1, 6, D), lambda b, t: (b, 0, 0)),
                  pl.BlockSpec((1, tq, nq), lambda b, t: (b, t, 0)),
                  pl.BlockSpec((1, seq, KV_DIM), lambda b, t: (b, 0, 0)),
                  pl.BlockSpec((1, seq, KV_DIM), lambda b, t: (b, 0, 0)),
                  pl.BlockSpec((1, past, KV_DIM), lambda b, t: (b, 0, 0)),
                  pl.BlockSpec((1, past, KV_DIM), lambda b, t: (b, 0, 0)),
                  _resident(wo.shape),
                  pl.BlockSpec(memory_space=pltpu.SMEM)],
        out_specs=pl.BlockSpec((1, tq, D), lambda b, t: (b, t, 0)),
        out_shape=jax.ShapeDtypeStruct((nb, seq, D), F32),
        scratch_shapes=[pltpu.VMEM((nq, tq), F32)],
        compiler_params=_params(2),
        name="attn_latent",
    )(x, mod, q, k, v, ck, cv, wo, sink)


def _dft_angle(k, t, period):
    return ((k * t) % period).astype(F32) * (2.0 * math.pi / period)


def _fourier_tables(seq):
    c = jnp.arange(FOURIER_GROUP_DIM, dtype=jnp.int32)
    ang_c = _dft_angle(c[:, None], c[None, :], FOURIER_GROUP_DIM)
    chan = jnp.concatenate([jnp.cos(ang_c), jnp.sin(ang_c)], axis=1).astype(BF16)
    k = jnp.arange(seq, dtype=jnp.int32)
    hi = jnp.arange(seq // LANES, dtype=jnp.int32) * LANES
    lo = jnp.arange(LANES, dtype=jnp.int32)
    ang_hi = _dft_angle(k[:, None], hi[None, :], seq)
    ang_lo = _dft_angle(k[:, None], lo[None, :], seq)
    return chan, jnp.cos(ang_hi), jnp.sin(ang_hi), jnp.cos(ang_lo), jnp.sin(ang_lo)


def _fourier_chan_kernel(x_ref, mod_ref, g_ref, chan_ref, o_ref, *, bb, tt):
    x = x_ref[...].reshape(bb * tt, D)
    mod = mod_ref[0]
    h = _modulate(x, g_ref[...], mod[0:1], mod[1:2]).astype(BF16)
    for gi in range(D // FOURIER_GROUP_DIM):
        cols = slice(gi * LANES, (gi + 1) * LANES)
        cs = _dot(h[:, cols], chan_ref[...]).astype(BF16)
        for b in range(bb):
            rows = slice(b * tt, (b + 1) * tt)
            o_ref[b, 0, :, cols] = cs[rows, 0:LANES]
            o_ref[b, 1, :, cols] = cs[rows, LANES:2 * LANES]


def _fourier_chan(x, mod, g, chan, *, bb, tt):
    nb, seq, _ = x.shape
    n_groups = mod.shape[0]
    return pl.pallas_call(
        functools.partial(_fourier_chan_kernel, bb=bb, tt=tt),
        grid=(nb // bb, seq // tt),
        in_specs=[pl.BlockSpec((bb, tt, D), lambda b, t: (b, t, 0)),
                  pl.BlockSpec((1, 6, D), lambda b, t: ((b * bb * n_groups) // nb, 0, 0)),
                  _resident((1, D)), _resident(chan.shape)],
        out_specs=pl.BlockSpec((bb, 2, tt, D), lambda b, t: (b, 0, t, 0)),
        out_shape=jax.ShapeDtypeStruct((nb, 2, seq, D), BF16),
        compiler_params=_params(2),
        name="fourier_chan",
    )(x, mod, g, chan)


def _fourier_pos_kernel(x_ref, mod_ref, hh_ref, chi_ref, shi_ref, clo_ref, slo_ref, wout_ref,
                        o_ref, lhs_ref, *, seq, scale):
    clo = clo_ref[...]
    slo = slo_ref[...]
    n_hi = seq // LANES
    blk = min(n_hi, DFT_BLOCK // LANES)
    f = None
    for a0 in range(0, n_hi, blk):
        for a in range(a0, a0 + blk):
            ca = chi_ref[:, a:a + 1]
            sa = shi_ref[:, a:a + 1]
            lhs_ref[:, a * LANES:(a + 1) * LANES] = (ca * clo - sa * slo).astype(BF16)
            lhs_ref[:, seq + a * LANES:seq + (a + 1) * LANES] = (
                -(sa * clo + ca * slo)).astype(BF16)
        for base in (0, seq):
            cols = slice(base + a0 * LANES, base + (a0 + blk) * LANES)
            part = _dot(lhs_ref[:, cols], hh_ref[0, cols, :])
            f = part if f is None else f + part
    f = (f * scale).astype(BF16)
    mod = mod_ref[0]
    o_ref[0] = x_ref[0] + mod[2:3] * _dot(f, wout_ref[...])


def _fourier_pos(x, mod, hh, chi, shi, clo, slo, wout, *, tr):
    nb, seq, _ = x.shape
    n_groups = mod.shape[0]
    n_hi = seq // LANES
    scale = 1.0 / math.sqrt(seq * FOURIER_GROUP_DIM)
    return pl.pallas_call(
        functools.partial(_fourier_pos_kernel, seq=seq, scale=scale),
        grid=(nb, seq // tr),
        in_specs=[pl.BlockSpec((1, tr, D), lambda b, t: (b, t, 0)),
                  pl.BlockSpec((1, 6, D), lambda b, t: ((b * n_groups) // nb, 0, 0)),
                  pl.BlockSpec((1, 2 * seq, D), lambda b, t: (b, 0, 0)),
                  pl.BlockSpec((tr, n_hi), lambda b, t: (t, 0)),
                  pl.BlockSpec((tr, n_hi), lambda b, t: (t, 0)),
                  pl.BlockSpec((tr, LANES), lambda b, t: (t, 0)),
                  pl.BlockSpec((tr, LANES), lambda b, t: (t, 0)),
                  _resident(wout.shape)],
        out_specs=pl.BlockSpec((1, tr, D), lambda b, t: (b, t, 0)),
        out_shape=jax.ShapeDtypeStruct((nb, seq, D), F32),
        scratch_shapes=[pltpu.VMEM((tr, 2 * seq), BF16)],
        compiler_params=_params(2),
        name="fourier_pos",
    )(x, mod, hh, chi, shi, clo, slo, wout)


def _fourier(x, mod, g, wout, *, bb, tt, tr):
    nb, seq, _ = x.shape
    chan, chi, shi, clo, slo = _fourier_tables(seq)
    hh = _fourier_chan(x, mod, g, chan, bb=bb, tt=tt)
    return _fourier_pos(x, mod, hh.reshape(nb, 2 * seq, D), chi, shi, clo, slo, wout, tr=tr)


def _fourier_short_kernel(x_ref, mod_ref, g_ref, chan_ref, chi_ref, shi_ref, clo_ref, slo_ref,
                          wout_ref, o_ref, lhs_ref, hh_ref, f_ref, *, bb, seq, scale):
    @pl.when(pl.program_id(0) == 0)
    def _():
        clo = clo_ref[...]
        slo = slo_ref[...]
        for a in range(seq // LANES):
            ca = chi_ref[:, a:a + 1]
            sa = shi_ref[:, a:a + 1]
            lhs_ref[:, a * LANES:(a + 1) * LANES] = (ca * clo - sa * slo).astype(BF16)
            lhs_ref[:, seq + a * LANES:seq + (a + 1) * LANES] = (-(sa * clo + ca * slo)).astype(BF16)

    x = x_ref[...]
    mod = mod_ref[0]
    h = _modulate(x, g_ref[...], mod[0:1], mod[1:2]).astype(BF16)
    for gi in range(D // FOURIER_GROUP_DIM):
        cols = slice(gi * LANES, (gi + 1) * LANES)
        cs = _dot(h[:, cols], chan_ref[...]).astype(BF16)
        for b in range(bb):
            rows = slice(b * seq, (b + 1) * seq)
            hh_ref[b, 0:seq, cols] = cs[rows, 0:LANES]
            hh_ref[b, seq:2 * seq, cols] = cs[rows, LANES:2 * LANES]
    for b in range(bb):
        f_ref[b * seq:(b + 1) * seq, :] = (_dot(lhs_ref[...], hh_ref[b]) * scale).astype(BF16)
    o_ref[...] = x + mod[2:3] * _dot(f_ref[...], wout_ref[...])


def _fourier_short_stage(g, wout, *, seq, tm):
    bb = tm // seq
    chan, chi, shi, clo, slo = _fourier_tables(seq)
    scale = 1.0 / math.sqrt(seq * FOURIER_GROUP_DIM)
    params = [g, chan, chi, shi, clo, slo, wout]
    return _Stage(functools.partial(_fourier_short_kernel, bb=bb, seq=seq, scale=scale),
                  [_resident(p.shape) for p in params], params,
                  [pltpu.VMEM((seq, 2 * seq), BF16), pltpu.VMEM((bb, 2 * seq, D), BF16),
                   pltpu.VMEM((tm, D), BF16)])


def kernel(x_prompt, x_sample, cache_k, cache_v, c, c_ctx, ada_w, ada_b, norm_mix_g, norm_ffn_g,
           final_g, attn_wqkv, attn_wo, attn_sink, sgu_w_in, sgu_ln_g, sgu_w_s, sgu_b_s,
           sgu_w_out, sc_w_in, sc_conv, sc_w_out, fn_w_out, ffn_w_up, ffn_conv, ffn_w_down):
    nbp, seq_p, _ = x_prompt.shape
    nbs, seq_s, _ = x_sample.shape
    depth = ada_w.shape[0]
    tm = TOKEN_TILE

    cvec = jnp.concatenate([c_ctx[None, :], c, jnp.zeros((SUBLANES - 1 - nbs, D), F32)], axis=0)
    mods = _ada_mod(cvec, ada_w, ada_b).reshape(depth, SUBLANES, 6, D)

    xp = x_prompt.reshape(nbp * seq_p, D)
    xs = x_sample.reshape(nbs * seq_s, D)
    new_k = new_v = None
    wup = ffn_w_up.astype(BF16)
    wdn = ffn_w_down.astype(BF16)
    g_ffn = norm_ffn_g[:, None, :]

    for l in range(depth):
        kind, j = l % 4, l // 4
        mp = mods[l, 0:1]
        ms = mods[l, 1:1 + nbs]
        g_mix = norm_mix_g[l][None, :]
        fg = final_g[None, :] if l == depth - 1 else None
        ffn_p = _ffn_stage(xp.shape[0], g_ffn, wup, ffn_conv, wdn, fg, layer=l, seq=seq_p, tm=tm)
        if kind == 0:
            wqkv = attn_wqkv[j].astype(BF16)
            wo = attn_wo[j].astype(BF16)
            sink = attn_sink[j][None, :]
            xp, new_k, new_v = _attn_prompt(xp, mp, g_mix, wqkv, wo, sink, seq=seq_p,
                                            bb=PROMPT_SEQS_PER_STEP)
            xp = _run_stages(xp, mp, [ffn_p], tm=tm, name="conv_ffn")
            cos, sin = _rope_tables(seq_s)
            q, k, v = _qkv_rope(xs, ms, g_mix, wqkv, cos, sin, seq=seq_s, tm=tm)
            past = cache_k.shape[2]
            ck = cache_k[:, j].reshape(nbs, past, KV_DIM).astype(BF16)
            cv = cache_v[:, j].reshape(nbs, past, KV_DIM).astype(BF16)
            xs = _attn_latent(xs.reshape(nbs, seq_s, D), ms, q.reshape(nbs, seq_s, N_HEADS * HD),
                              k.reshape(nbs, seq_s, KV_DIM), v.reshape(nbs, seq_s, KV_DIM),
                              ck, cv, wo, sink, tq=QUERY_TILE).reshape(nbs * seq_s, D)
        elif kind == 1:
            win = sgu_w_in[j].astype(BF16)
            ws = sgu_w_s[j].astype(BF16)
            wout = sgu_w_out[j].astype(BF16)
            lng = sgu_ln_g[j][None, :]
            bs_full = jnp.repeat(sgu_b_s[j].T, LANES, axis=1)
            sgu = _sgu_stage(g_mix, win, lng, ws, bs_full, wout, tm=tm)
            xp = _run_stages(xp, mp, [sgu, ffn_p], tm=tm, name="sgu_ffn")
            xs = _run_stages(xs, ms, [sgu], tm=tm, name="sgu")
        elif kind == 2:
            win = sc_w_in[j].astype(BF16)
            wout = sc_w_out[j].astype(BF16)
            sconv_p = _sconv_stage(xp.shape[0], g_mix, win, sc_conv[j], wout, seq=seq_p, tm=tm)
            xp = _run_stages(xp, mp, [sconv_p, ffn_p], tm=tm, name="short_conv_ffn")
            xs = _sconv(xs, ms, g_mix, win, sc_conv[j], wout, seq=seq_s, tm=tm)
        else:
            wout = fn_w_out[j].astype(BF16)
            fnet_p = _fourier_short_stage(g_mix, wout, seq=seq_p, tm=tm)
            xp = _run_stages(xp, mp, [fnet_p, ffn_p], tm=tm, name="fourier_ffn")
            xs = _fourier(xs.reshape(nbs, seq_s, D), ms, g_mix, wout,
                          bb=1, tt=tm, tr=tm).reshape(nbs * seq_s, D)

        xs = _ffn(xs, ms, g_ffn, wup, ffn_conv, wdn, fg, layer=l, seq=seq_s, tm=tm)

    y_prompt = xp.reshape(nbp, seq_p, D)
    y_sample = xs.reshape(nbs, seq_s, D)
    new_cache_k = new_k.reshape(nbp, 1, seq_p, N_KV, HD)
    new_cache_v = new_v.reshape(nbp, 1, seq_p, N_KV, HD)
    return (y_prompt, y_sample, new_cache_k, new_cache_v)
```
